```python
import math
import numpy as np
import jax
import jax.numpy as jnp
from jax import lax

D_MODEL = 1024
BATCH = 8
SEQ = 4096
DEPTH = 2

GRID_W = 64
CTX_LEN = 256
HEAD_DIM = 64
D_CONV = 512
CONV_K = 31
NA_HEADS = 8
NA_MAX_ROWS = 8
NA_COLS = 16
WA_HEADS = 8
WA_KV_HEADS = 2
WA_WINDOW = 128
WA_BLOCK = 128
ROPE_BASE = 10000.0
EPS = 1e-6
D_NA = NA_HEADS * HEAD_DIM
D_WA = WA_HEADS * HEAD_DIM
D_WA_KV = WA_KV_HEADS * HEAD_DIM
SPLIT_SIZES = (2 * D_CONV, D_CONV, D_NA, D_NA, D_NA, D_NA, D_WA, D_WA_KV, D_WA_KV, D_WA, 3 * D_MODEL)
D_IN = sum(SPLIT_SIZES)
SPLIT_POINTS = tuple(int(v) for v in np.cumsum(SPLIT_SIZES)[:-1])
SPLIT_STARTS = (0,) + SPLIT_POINTS

kernel_name = 'hybrid_conv_natten_swa_prefix_dit_block'


def rms_norm(x, g):
    xf = x.astype(jnp.float32)
    y = xf * lax.rsqrt(jnp.mean(xf * xf, axis=-1, keepdims=True) + EPS)
    return (y * g.astype(jnp.float32)).astype(x.dtype)


def layer_norm(x, g, b):
    xf = x.astype(jnp.float32)
    mu = jnp.mean(xf, axis=-1, keepdims=True)
    var = jnp.mean(jnp.square(xf - mu), axis=-1, keepdims=True)
    y = (xf - mu) * lax.rsqrt(var + EPS)
    return (y * g.astype(jnp.float32) + b.astype(jnp.float32)).astype(x.dtype)


def depthwise_conv(x, w, b):
    y = lax.conv_general_dilated(
        x, w[:, None, :].astype(x.dtype), window_strides=(1,),
        padding=[(CONV_K // 2, CONV_K // 2)],
        dimension_numbers=('NWC', 'WIO', 'NWC'),
        feature_group_count=x.shape[-1])
    return y + b


def conv_branch(u_glu, u_gate, conv_w, conv_b, ln_g, ln_b, w_proj):
    a = u_glu[..., :D_CONV] * jax.nn.sigmoid(u_glu[..., D_CONV:])
    a = depthwise_conv(a, conv_w, conv_b)
    a = jax.nn.silu(layer_norm(a, ln_g, ln_b))
    return (a * jax.nn.silu(u_gate)) @ w_proj


def split_heads(t, n):
    return t.reshape(t.shape[0], t.shape[1], n, HEAD_DIM)


def rotate(x, pos):
    half = x.shape[-1] // 2
    inv = ROPE_BASE ** (-jnp.arange(half, dtype=jnp.float32) / half)
    ang = pos.astype(jnp.float32)[:, None] * inv[None, :]
    cos = jnp.cos(ang)[None, :, None, :]
    sin = jnp.sin(ang)[None, :, None, :]
    x1 = x[..., :half].astype(jnp.float32)
    x2 = x[..., half:].astype(jnp.float32)
    return jnp.concatenate([x1 * cos - x2 * sin, x1 * sin + x2 * cos], axis=-1).astype(x.dtype)


def axial_rope(t, row_pos, col_pos):
    half = HEAD_DIM // 2
    return jnp.concatenate([rotate(t[..., :half], row_pos), rotate(t[..., half:], col_pos)], axis=-1)


def sink_softmax(s, sink):
    m = jnp.maximum(jnp.max(s, axis=-1, keepdims=True), sink)
    e = jnp.exp(s - m)
    return e / (jnp.sum(e, axis=-1, keepdims=True) + jnp.exp(sink - m))


def context_attention(q, k, v, sink):
    s = jnp.einsum('bqkgd,bnkd->bkgqn', q, k).astype(jnp.float32) * HEAD_DIM ** -0.5
    if sink is None:
        p = jax.nn.softmax(s, axis=-1)
    else:
        p = sink_softmax(s, sink.astype(jnp.float32)[None, :, :, None, None])
    o = jnp.einsum('bkgqn,bnkd->bqkgd', p.astype(v.dtype), v)
    return o.reshape(o.shape[0], o.shape[1], -1)


def neighbourhood_attention(q, k, v, k_ctx, v_ctx, rpb):
    B, S = q.shape[0], q.shape[1]
    rows = S // GRID_W
    kr = min(NA_MAX_ROWS, rows)
    scale = HEAD_DIM ** -0.5
    grid = (B, rows, GRID_W, NA_HEADS, HEAD_DIM)
    qg, kg, vg = q.reshape(grid), k.reshape(grid), v.reshape(grid)
    col_start = np.clip(np.arange(GRID_W) - NA_COLS // 2, 0, GRID_W - NA_COLS)
    col_idx = col_start[:, None] + np.arange(NA_COLS)[None, :]
    col_off = col_idx - np.arange(GRID_W)[:, None] + (NA_COLS - 1)
    rpb_cols = rpb[:, :, col_off]

    def row_fn(i):
        rs = jnp.clip(i - kr // 2, 0, rows - kr)
        q_i = lax.dynamic_index_in_dim(qg, i, axis=1, keepdims=False)
        k_w = lax.dynamic_slice_in_dim(kg, rs, kr, axis=1)[:, :, col_idx]
        v_w = lax.dynamic_slice_in_dim(vg, rs, kr, axis=1)[:, :, col_idx]
        row_off = rs + jnp.arange(kr) - i + (NA_MAX_ROWS - 1)
        bias = jnp.transpose(rpb_cols[:, row_off], (0, 2, 1, 3)).astype(jnp.float32)
        s_w = jnp.einsum('bjhd,brjchd->bhjrc', q_i, k_w).astype(jnp.float32) * scale + bias[None]
        s_c = jnp.einsum('bjhd,bnhd->bhjn', q_i, k_ctx).astype(jnp.float32) * scale
        s = jnp.concatenate([s_w.reshape(B, NA_HEADS, GRID_W, kr * NA_COLS), s_c], axis=-1)
        p = jax.nn.softmax(s, axis=-1).astype(v.dtype)
        p_w = p[..., :kr * NA_COLS].reshape(B, NA_HEADS, GRID_W, kr, NA_COLS)
        o = (jnp.einsum('bhjrc,brjchd->bjhd', p_w, v_w)
             + jnp.einsum('bhjn,bnhd->bjhd', p[..., kr * NA_COLS:], v_ctx))
        return o

    out = lax.map(row_fn, jnp.arange(rows))
    return jnp.moveaxis(out, 0, 1).reshape(B, S, NA_HEADS * HEAD_DIM)


def window_attention(q, k, v, k_ctx, v_ctx, sink):
    B, S = q.shape[0], q.shape[1]
    G = WA_HEADS // WA_KV_HEADS
    nb = S // WA_BLOCK
    span = WA_BLOCK + 2 * WA_WINDOW
    scale = HEAD_DIM ** -0.5
    qb = q.reshape(B, nb, WA_BLOCK, WA_KV_HEADS, G, HEAD_DIM)
    pad = ((0, 0), (WA_WINDOW, WA_WINDOW), (0, 0), (0, 0))
    kp = jnp.pad(k, pad)
    vp = jnp.pad(v, pad)
    rel = (jnp.arange(span) - WA_WINDOW)[None, :] - jnp.arange(WA_BLOCK)[:, None]
    in_window = jnp.abs(rel) <= WA_WINDOW
    sink_b = sink.astype(jnp.float32).reshape(1, WA_KV_HEADS, G, 1, 1)

    def block_fn(n):
        start = n * WA_BLOCK
        q_n = lax.dynamic_index_in_dim(qb, n, axis=1, keepdims=False)
        k_n = lax.dynamic_slice_in_dim(kp, start, span, axis=1)
        v_n = lax.dynamic_slice_in_dim(vp, start, span, axis=1)
        kpos = start - WA_WINDOW + jnp.arange(span)
        valid = in_window & ((kpos >= 0) & (kpos < S))[None, :]
        s_w = jnp.einsum('bqkgd,bskd->bkgqs', q_n, k_n).astype(jnp.float32) * scale
        s_w = jnp.where(valid, s_w, -jnp.inf)
        s_c = jnp.einsum('bqkgd,bnkd->bkgqn', q_n, k_ctx).astype(jnp.float32) * scale
        p = sink_softmax(jnp.concatenate([s_w, s_c], axis=-1), sink_b).astype(v.dtype)
        o = (jnp.einsum('bkgqs,bskd->bqkgd', p[..., :span], v_n)
             + jnp.einsum('bkgqn,bnkd->bqkgd', p[..., span:], v_ctx))
        return o

    out = lax.map(block_fn, jnp.arange(nb))
    return jnp.moveaxis(out, 0, 1).reshape(B, S, WA_HEADS * HEAD_DIM)


def merge_branches(a_glu, a_gate, o_na, n_gate, o_wa, w_gate, merge,
                   conv_w, conv_b, cln_g, cln_b, w_proj_a, w_proj_b, w_proj_c, w_o):
    y_a = conv_branch(a_glu, a_gate, conv_w, conv_b, cln_g, cln_b, w_proj_a)
    y_b = (o_na * jax.nn.silu(n_gate)) @ w_proj_b
    y_c = (o_wa * jax.nn.silu(w_gate)) @ w_proj_c
    g_a, g_b, g_c = jnp.split(merge, 3, axis=-1)
    y = jax.nn.sigmoid(g_a) * y_a + jax.nn.sigmoid(g_b) * y_b + jax.nn.sigmoid(g_c) * y_c
    return y @ w_o


def context_kv(hc, w_in_l):
    st = SPLIT_STARTS
    nk, nv = jnp.split(hc @ w_in_l[:, st[3]:st[5]], 2, axis=-1)
    wk, wv = jnp.split(hc @ w_in_l[:, st[7]:st[9]], 2, axis=-1)
    return nk, nv, wk, wv


def setup_inputs(seed: int = 0) -> dict:
    key = jax.random.key(seed)
    ks = jax.random.split(key, 24)
    L = DEPTH

    def nrm(k, shape, s):
        return jax.random.normal(k, shape, jnp.float32) * s

    return {
        'x': nrm(ks[0], (BATCH, SEQ, D_MODEL), 1.0),
        'c': nrm(ks[1], (BATCH, D_MODEL), 1.0),
        'ctx': nrm(ks[2], (BATCH, CTX_LEN, D_MODEL), 1.0),
        'c_ctx': nrm(ks[3], (D_MODEL,), 1.0),
        'norm_g': 1.0 + nrm(ks[4], (L, D_MODEL), 0.02),
        'w_ada': nrm(ks[5], (L, D_MODEL, 3 * D_MODEL), 0.5 * D_MODEL ** -0.5),
        'b_ada': nrm(ks[6], (L, 3 * D_MODEL), 0.02),
        'w_in': nrm(ks[7], (L, D_MODEL, D_IN), D_MODEL ** -0.5),
        'conv_w': nrm(ks[8], (L, CONV_K, D_CONV), CONV_K ** -0.5),
        'conv_b': nrm(ks[9], (L, D_CONV), 0.02),
        'cln_g': 1.0 + nrm(ks[10], (L, D_CONV), 0.02),
        'cln_b': nrm(ks[11], (L, D_CONV), 0.02),
        'w_proj_a': nrm(ks[12], (L, D_CONV, D_MODEL), D_CONV ** -0.5),
        'na_q_norm': 1.0 + nrm(ks[13], (L, HEAD_DIM), 0.02),
        'na_k_norm': 1.0 + nrm(ks[14], (L, HEAD_DIM), 0.02),
        'na_rpb': nrm(ks[15], (L, NA_HEADS, 2 * NA_MAX_ROWS - 1, 2 * NA_COLS - 1), 0.1),
        'w_proj_b': nrm(ks[16], (L, D_NA, D_MODEL), D_NA ** -0.5),
        'wa_q_norm': 1.0 + nrm(ks[17], (L, HEAD_DIM), 0.02),
        'wa_k_norm': 1.0 + nrm(ks[18], (L, HEAD_DIM), 0.02),
        'wa_sink': nrm(ks[19], (L, WA_HEADS), 1.0),
        'w_proj_c': nrm(ks[20], (L, D_WA, D_MODEL), D_WA ** -0.5),
        'w_o': nrm(ks[21], (L, D_MODEL, D_MODEL), D_MODEL ** -0.5),
    }


def reference(x, c, ctx, c_ctx, norm_g, w_ada, b_ada, w_in, conv_w, conv_b, cln_g, cln_b,
              w_proj_a, na_q_norm, na_k_norm, na_rpb, w_proj_b, wa_q_norm, wa_k_norm,
              wa_sink, w_proj_c, w_o):
    B, S, _ = x.shape
    N = ctx.shape[1]
    G = WA_HEADS // WA_KV_HEADS
    t = jnp.arange(S)
    row_pos = t // GRID_W
    col_pos = t % GRID_W
    for l in range(DEPTH):
        update_ctx = l < DEPTH - 1
        mod = jax.nn.silu(c) @ w_ada[l] + b_ada[l]
        mod_c = jax.nn.silu(c_ctx) @ w_ada[l] + b_ada[l]
        shift, scale, gate = jnp.split(mod[:, None, :], 3, axis=-1)
        shift_c, scale_c, gate_c = jnp.split(mod_c[None, None, :], 3, axis=-1)
        h = rms_norm(x, norm_g[l]) * (1.0 + scale) + shift
        hc = rms_norm(ctx, norm_g[l]) * (1.0 + scale_c) + shift_c

        (a_glu, a_gate, nq, nk, nv, n_gate, wq, wk, wv, w_gate, merge) = jnp.split(
            h @ w_in[l], SPLIT_POINTS, axis=-1)
        if update_ctx:
            cp = jnp.split(hc @ w_in[l], SPLIT_POINTS, axis=-1)
            nck, ncv, wck, wcv = cp[3], cp[4], cp[7], cp[8]
        else:
            nck, ncv, wck, wcv = context_kv(hc, w_in[l])

        nk_c = rms_norm(split_heads(nck, NA_HEADS), na_k_norm[l])
        nv_c = split_heads(ncv, NA_HEADS)
        wk_c = rms_norm(split_heads(wck, WA_KV_HEADS), wa_k_norm[l])
        wv_c = split_heads(wcv, WA_KV_HEADS)

        o_na = neighbourhood_attention(
            rms_norm(split_heads(nq, NA_HEADS), na_q_norm[l]),
            rms_norm(split_heads(nk, NA_HEADS), na_k_norm[l]),
            split_heads(nv, NA_HEADS), nk_c, nv_c, na_rpb[l])
        o_wa = window_attention(
            axial_rope(rms_norm(split_heads(wq, WA_HEADS), wa_q_norm[l]), row_pos, col_pos),
            axial_rope(rms_norm(split_heads(wk, WA_KV_HEADS), wa_k_norm[l]), row_pos, col_pos),
            split_heads(wv, WA_KV_HEADS), wk_c, wv_c, wa_sink[l])
        y = merge_branches(a_glu, a_gate, o_na, n_gate, o_wa, w_gate, merge,
                           conv_w[l], conv_b[l], cln_g[l], cln_b[l],
                           w_proj_a[l], w_proj_b[l], w_proj_c[l], w_o[l])
        x_new = x + gate * y

        if update_ctx:
            (ac_glu, ac_gate, ncq, _, _, nc_gate, wcq, _, _, wc_gate, merge_c) = cp
            q_na_c = rms_norm(split_heads(ncq, NA_HEADS), na_q_norm[l])[:, :, :, None, :]
            oc_na = context_attention(q_na_c, nk_c, nv_c, None)
            q_wa_c = rms_norm(split_heads(wcq, WA_HEADS), wa_q_norm[l]).reshape(
                B, N, WA_KV_HEADS, G, HEAD_DIM)
            oc_wa = context_attention(q_wa_c, wk_c, wv_c, wa_sink[l].reshape(WA_KV_HEADS, G))
            yc = merge_branches(ac_glu, ac_gate, oc_na, nc_gate, oc_wa, wc_gate, merge_c,
                                conv_w[l], conv_b[l], cln_g[l], cln_b[l],
                                w_proj_a[l], w_proj_b[l], w_proj_c[l], w_o[l])
            ctx = ctx + gate_c * yc
        x = x_new
    return x
```

```python
import functools

import numpy as np
import jax
import jax.numpy as jnp
from jax import lax
from jax.experimental import pallas as pl
from jax.experimental.pallas import tpu as pltpu

D_MODEL = 1024
GRID_W = 64
HEAD_DIM = 64
D_CONV = 512
CONV_K = 31
NA_HEADS = 8
NA_MAX_ROWS = 8
NA_COLS = 16
WA_HEADS = 8
WA_KV_HEADS = 2
WA_GROUP = WA_HEADS // WA_KV_HEADS
WA_WINDOW = 128
WA_BLOCK = 128
ROPE_BASE = 10000.0
EPS = 1e-6
D_NA = NA_HEADS * HEAD_DIM
D_WA = WA_HEADS * HEAD_DIM
D_WA_KV = WA_KV_HEADS * HEAD_DIM
SPLIT_SIZES = (2 * D_CONV, D_CONV, D_NA, D_NA, D_NA, D_NA, D_WA, D_WA_KV, D_WA_KV, D_WA, 3 * D_MODEL)
D_IN = sum(SPLIT_SIZES)
SPLIT_STARTS = tuple(int(v) for v in np.cumsum((0,) + SPLIT_SIZES[:-1]))

LANES = 128
MXU_DIM = 256
HEAD_PAIR = 2 * HEAD_DIM
ROPE_HALF = HEAD_DIM // 4
NEG_BIG = -1e30
VMEM_LIMIT = 56 * 1024 * 1024

F32 = jnp.float32
BF16 = jnp.bfloat16


def _sigmoid(v):
    return 1.0 / (1.0 + jnp.exp(-v))


def _silu(v):
    return v * _sigmoid(v)


def _params(n_grid):
    return pltpu.CompilerParams(dimension_semantics=("arbitrary",) * n_grid,
                                vmem_limit_bytes=VMEM_LIMIT)


def _mod_kernel(c_ref, w_ref, b_ref, o_ref):
    s = _silu(c_ref[...])
    o_ref[...] = jnp.dot(s.astype(BF16), w_ref[...].astype(BF16),
                         preferred_element_type=F32) + b_ref[...]


def _modulation(c_all, w_ada, b_ada):
    rows = c_all.shape[0]
    n_out = w_ada.shape[1]
    blk = D_MODEL
    return pl.pallas_call(
        _mod_kernel,
        grid=(n_out // blk,),
        in_specs=[pl.BlockSpec((rows, D_MODEL), lambda j: (0, 0)),
                  pl.BlockSpec((D_MODEL, blk), lambda j: (0, j)),
                  pl.BlockSpec((1, blk), lambda j: (0, j))],
        out_specs=pl.BlockSpec((rows, blk), lambda j: (0, j)),
        out_shape=jax.ShapeDtypeStruct((rows, n_out), F32),
        compiler_params=_params(1),
        name="modulation",
    )(c_all, w_ada, b_ada)


def _proj_kernel(*refs, rope):
    if rope:
        (x_ref, mod_ref, g_ref, w_ref, nqw_ref, nkw_ref, wqw_ref, wkw_ref, bd_ref,
         cos_ref, s1_ref, s2_ref) = refs[:12]
        outs = refs[12:]
    else:
        (x_ref, mod_ref, g_ref, w_ref, nqw_ref, nkw_ref, wqw_ref, wkw_ref, bd_ref) = refs[:9]
        cos_ref = s1_ref = s2_ref = None
        outs = refs[9:]
    (a_ref, ga_ref, nq_ref, nk_ref, nv_ref, gn_ref, wq_ref, wk_ref, wv_ref, gw_ref, mg_ref,
     h_scr) = outs

    x = x_ref[0]
    ms = jnp.mean(x * x, axis=-1, keepdims=True)
    shift = mod_ref[0, :, 0:D_MODEL]
    scale = mod_ref[0, :, D_MODEL:2 * D_MODEL]
    h = (x * lax.rsqrt(ms + EPS) * g_ref[...]) * (1.0 + scale) + shift
    h_scr[...] = h.astype(BF16)

    st = SPLIT_STARTS

    def seg(c0, n):
        return jnp.dot(h_scr[...], w_ref[:, c0:c0 + n], preferred_element_type=F32)

    def head_norm(t, w_row_ref):
        n = t.shape[1]
        sq = (t * t).astype(BF16)
        if n >= MXU_DIM:
            parts = [jnp.dot(sq[:, c:c + MXU_DIM], bd_ref[...], preferred_element_type=F32)
                     for c in range(0, n, MXU_DIM)]
            ss = jnp.concatenate(parts, axis=1) if len(parts) > 1 else parts[0]
        else:
            ss = jnp.dot(sq, bd_ref[0:n, 0:n], preferred_element_type=F32)
        return t * lax.rsqrt(ss * (1.0 / HEAD_DIM) + EPS) * w_row_ref[...]

    def store_rot(t, o_ref):
        n = t.shape[1]
        for c in range(0, n, LANES):
            tc = t[:, c:c + LANES]
            if rope:
                lo = pltpu.roll(tc, ROPE_HALF, axis=1)
                hi = pltpu.roll(tc, LANES - ROPE_HALF, axis=1)
                tc = tc * cos_ref[...] + lo * s1_ref[...] + hi * s2_ref[...]
            o_ref[0, :, c:c + LANES] = tc.astype(BF16)

    glu = seg(st[0], 2 * D_CONV)
    a_ref[0] = (glu[:, :D_CONV] * _sigmoid(glu[:, D_CONV:])).astype(BF16)
    ga_ref[0] = _silu(seg(st[1], D_CONV)).astype(BF16)
    nq_ref[0] = head_norm(seg(st[2], D_NA), nqw_ref).astype(BF16)
    nk_ref[0] = head_norm(seg(st[3], D_NA), nkw_ref).astype(BF16)
    nv_ref[0] = seg(st[4], D_NA).astype(BF16)
    gn_ref[0] = _silu(seg(st[5], D_NA)).astype(BF16)
    store_rot(head_norm(seg(st[6], D_WA), wqw_ref), wq_ref)
    store_rot(head_norm(seg(st[7], D_WA_KV), wkw_ref), wk_ref)
    wv_ref[0] = seg(st[8], D_WA_KV).astype(BF16)
    gw_ref[0] = _silu(seg(st[9], D_WA)).astype(BF16)
    for c in range(0, 3 * D_MODEL, D_CONV):
        mg_ref[0, :, c:c + D_CONV] = _sigmoid(seg(st[10] + c, D_CONV)).astype(BF16)


_PROJ_OUT_WIDTHS = (D_CONV, D_CONV, D_NA, D_NA, D_NA, D_NA, D_WA, D_WA_KV, D_WA_KV, D_WA, 3 * D_MODEL)


def _projection(xs, mod, norm_g, w_in, nqw, nkw, wqw, wkw, bd, rope_tabs, tm):
    B, T, _ = xs.shape
    rope = rope_tabs is not None
    const2 = lambda b, t: (0, 0)
    row_spec = lambda n: pl.BlockSpec((1, n), const2)
    in_specs = [
        pl.BlockSpec((1, tm, D_MODEL), lambda b, t: (b, t, 0)),
        pl.BlockSpec((1, 1, 3 * D_MODEL), lambda b, t: (b, 0, 0)),
        row_spec(D_MODEL),
        pl.BlockSpec((D_MODEL, D_IN), const2, pipeline_mode=pl.Buffered(1)),
        row_spec(D_NA), row_spec(D_NA), row_spec(D_WA), row_spec(D_WA_KV),
        pl.BlockSpec((MXU_DIM, MXU_DIM), const2),
    ]
    args = [xs, mod, norm_g, w_in, nqw, nkw, wqw, wkw, bd]
    if rope:
        in_specs += [pl.BlockSpec((tm, LANES), lambda b, t: (t, 0))] * 3
        args += list(rope_tabs)
    out_specs = [pl.BlockSpec((1, tm, n), lambda b, t: (b, t, 0)) for n in _PROJ_OUT_WIDTHS]
    out_shape = [jax.ShapeDtypeStruct((B, T, n), BF16) for n in _PROJ_OUT_WIDTHS]
    return pl.pallas_call(
        functools.partial(_proj_kernel, rope=rope),
        grid=(B, T // tm),
        in_specs=in_specs,
        out_specs=out_specs,
        out_shape=out_shape,
        scratch_shapes=[pltpu.VMEM((tm, D_MODEL), BF16)],
        compiler_params=_params(2),
        name="projection_rope" if rope else "projection_ctx",
    )(*args)


def _nt_dot(a, b):
    return lax.dot_general(a, b, (((1,), (1,)), ((), ())), preferred_element_type=F32)


def _low_half(rows):
    return lax.broadcasted_iota(jnp.int32, (rows, HEAD_PAIR), 1) < HEAD_DIM


def _keep_half(q_pair, half):
    lane = lax.broadcasted_iota(jnp.int32, (1, HEAD_PAIR), 1)
    keep = (lane < HEAD_DIM) if half == 0 else (lane >= HEAD_DIM)
    return q_pair * keep.astype(F32).astype(q_pair.dtype)


def _softmax_pv(score_parts, value_parts, sink_col=None):
    m = None
    for s in score_parts:
        mi = jnp.max(s, axis=-1, keepdims=True)
        m = mi if m is None else jnp.maximum(m, mi)
    if sink_col is not None:
        m = jnp.maximum(m, sink_col)
    den = None if sink_col is None else jnp.exp(sink_col - m)
    acc = None
    for s, v in zip(score_parts, value_parts):
        p = jnp.exp(s - m)
        li = jnp.sum(p, axis=-1, keepdims=True)
        den = li if den is None else den + li
        o = jnp.dot(p.astype(BF16), v, preferred_element_type=F32)
        acc = o if acc is None else acc + o
    return acc, 1.0 / den


def _na_kernel(q_ref, k_ref, v_ref, kc_ref, vc_ref, bias_ref, o_ref, *, rows_per_step, n_rows):
    blk = pl.program_id(1)
    low = _low_half(GRID_W)
    win = NA_MAX_ROWS * GRID_W

    def row_body(r, carry):
        i = blk * rows_per_step + r
        rs = jnp.clip(i - NA_MAX_ROWS // 2, 0, n_rows - NA_MAX_ROWS)
        start = rs - i + (NA_MAX_ROWS - 1)
        q0 = pl.multiple_of(r * GRID_W, GRID_W)
        k0 = pl.multiple_of(rs * GRID_W, GRID_W)
        for hp in range(NA_HEADS // 2):
            lanes = slice(hp * HEAD_PAIR, (hp + 1) * HEAD_PAIR)
            q_pair = q_ref[0, pl.ds(q0, GRID_W), lanes]
            k_w = k_ref[0, pl.ds(k0, win), lanes]
            v_w = v_ref[0, pl.ds(k0, win), lanes]
            k_c = kc_ref[0, :, lanes]
            v_c = vc_ref[0, :, lanes]
            halves = []
            for half in range(2):
                qh = _keep_half(q_pair, half)
                s_w = _nt_dot(qh, k_w) + bias_ref[start, 2 * hp + half]
                s_c = _nt_dot(qh, k_c)
                o, inv = _softmax_pv((s_w, s_c), (v_w, v_c))
                halves.append(o * inv)
            o_ref[0, pl.ds(q0, GRID_W), lanes] = jnp.where(low, halves[0], halves[1]).astype(BF16)
        return carry

    lax.fori_loop(0, rows_per_step, row_body, 0)


def _na_attention(q, k, v, kc, vc, bias):
    B, S, _ = q.shape
    N = kc.shape[1]
    n_rows = S // GRID_W
    assert n_rows >= NA_MAX_ROWS and S % GRID_W == 0
    rows_per_step = 8
    assert n_rows % rows_per_step == 0
    tq = rows_per_step * GRID_W
    full = lambda b, t: (b, 0, 0)
    return pl.pallas_call(
        functools.partial(_na_kernel, rows_per_step=rows_per_step, n_rows=n_rows),
        grid=(B, n_rows // rows_per_step),
        in_specs=[pl.BlockSpec((1, tq, D_NA), lambda b, t: (b, t, 0)),
                  pl.BlockSpec((1, S, D_NA), full),
                  pl.BlockSpec((1, S, D_NA), full),
                  pl.BlockSpec((1, N, D_NA), full),
                  pl.BlockSpec((1, N, D_NA), full),
                  pl.BlockSpec(bias.shape, lambda b, t: (0, 0, 0, 0))],
        out_specs=pl.BlockSpec((1, tq, D_NA), lambda b, t: (b, t, 0)),
        out_shape=jax.ShapeDtypeStruct((B, S, D_NA), BF16),
        compiler_params=_params(2),
        name="na_attention",
    )(q, k, v, kc, vc, bias)


def _na_bias_table(rpb):
    j = np.arange(GRID_W)[:, None]
    kc = np.arange(GRID_W)[None, :]
    cs = np.clip(j - NA_COLS // 2, 0, GRID_W - NA_COLS)
    in_win = (kc >= cs) & (kc < cs + NA_COLS)
    c_off = np.clip(kc - j + (NA_COLS - 1), 0, 2 * NA_COLS - 2)
    tb = jnp.where(in_win[None, None], rpb[:, :, c_off].astype(F32), NEG_BIG)
    tabs = [tb[:, s0:s0 + NA_MAX_ROWS].transpose(0, 2, 1, 3).reshape(NA_HEADS, GRID_W, NA_MAX_ROWS * GRID_W)
            for s0 in range(NA_MAX_ROWS)]
    return jnp.stack(tabs)


def _wa_kernel(sink_ref, q_ref, k_ref, v_ref, kc_ref, vc_ref, o_ref, *, seq):
    n = pl.program_id(1)
    span = WA_BLOCK + 2 * WA_WINDOW
    ks = jnp.clip(n * WA_BLOCK - WA_WINDOW, 0, seq - span)
    k0 = pl.multiple_of(ks, WA_BLOCK)
    k_w = k_ref[0, pl.ds(k0, span), :]
    v_w = v_ref[0, pl.ds(k0, span), :]
    k_c = kc_ref[0]
    v_c = vc_ref[0]
    m_rows = WA_GROUP * WA_BLOCK
    row = lax.broadcasted_iota(jnp.int32, (m_rows, span), 0)
    col = lax.broadcasted_iota(jnp.int32, (m_rows, span), 1)
    rel = col + (ks - n * WA_BLOCK) - (row & (WA_BLOCK - 1))
    in_window = jnp.abs(rel) <= WA_WINDOW
    low = _low_half(WA_BLOCK)
    out_full = []
    for kh in range(WA_KV_HEADS):
        q_parts = []
        for g in range(WA_GROUP):
            qb = q_ref[0, :, g * HEAD_PAIR:(g + 1) * HEAD_PAIR]
            q_parts.append(_keep_half(qb, kh))
        q_stack = jnp.concatenate(q_parts, axis=0)
        sink_col = jnp.concatenate(
            [jnp.full((WA_BLOCK, 1), sink_ref[kh * WA_GROUP + g], F32) for g in range(WA_GROUP)], axis=0)
        s_w = jnp.where(in_window, _nt_dot(q_stack, k_w), NEG_BIG)
        s_c = _nt_dot(q_stack, k_c)
        o, inv = _softmax_pv((s_w, s_c), (v_w, v_c), sink_col)
        out_full.append(o * inv)
    for g in range(WA_GROUP):
        rows = slice(g * WA_BLOCK, (g + 1) * WA_BLOCK)
        o_ref[0, :, g * HEAD_PAIR:(g + 1) * HEAD_PAIR] = jnp.where(
            low, out_full[0][rows], out_full[1][rows]).astype(BF16)


def _wa_attention(q, k, v, kc, vc, sink):
    B, S, _ = q.shape
    N = kc.shape[1]
    assert S % WA_BLOCK == 0 and S >= WA_BLOCK + 2 * WA_WINDOW
    full = lambda b, t: (b, 0, 0)
    return pl.pallas_call(
        functools.partial(_wa_kernel, seq=S),
        grid=(B, S // WA_BLOCK),
        in_specs=[pl.BlockSpec(memory_space=pltpu.SMEM),
                  pl.BlockSpec((1, WA_BLOCK, D_WA), lambda b, t: (b, t, 0)),
                  pl.BlockSpec((1, S, D_WA_KV), full),
                  pl.BlockSpec((1, S, D_WA_KV), full),
                  pl.BlockSpec((1, N, D_WA_KV), full),
                  pl.BlockSpec((1, N, D_WA_KV), full)],
        out_specs=pl.BlockSpec((1, WA_BLOCK, D_WA), lambda b, t: (b, t, 0)),
        out_shape=jax.ShapeDtypeStruct((B, S, D_WA), BF16),
        compiler_params=_params(2),
        name="wa_attention",
    )(sink, q, k, v, kc, vc)


def _ctx_attn_kernel(sink_ref, nq_ref, nk_ref, nv_ref, wq_ref, wk_ref, wv_ref, ona_ref, owa_ref):
    n_ctx = nq_ref.shape[1]
    low = _low_half(n_ctx)
    for hp in range(NA_HEADS // 2):
        lanes = slice(hp * HEAD_PAIR, (hp + 1) * HEAD_PAIR)
        q_pair = nq_ref[0, :, lanes]
        k_c = nk_ref[0, :, lanes]
        v_c = nv_ref[0, :, lanes]
        halves = []
        for half in range(2):
            qh = _keep_half(q_pair, half)
            o, inv = _softmax_pv((_nt_dot(qh, k_c),), (v_c,))
            halves.append(o * inv)
        ona_ref[0, :, lanes] = jnp.where(low, halves[0], halves[1]).astype(BF16)
    k_c = wk_ref[0]
    v_c = wv_ref[0]
    for g in range(WA_GROUP):
        lanes = slice(g * HEAD_PAIR, (g + 1) * HEAD_PAIR)
        q_pair = wq_ref[0, :, lanes]
        halves = []
        for kh in range(WA_KV_HEADS):
            qh = _keep_half(q_pair, kh)
            sink_col = jnp.full((n_ctx, 1), sink_ref[kh * WA_GROUP + g], F32)
            o, inv = _softmax_pv((_nt_dot(qh, k_c),), (v_c,), sink_col)
            halves.append(o * inv)
        owa_ref[0, :, lanes] = jnp.where(low, halves[0], halves[1]).astype(BF16)


def _ctx_attention(nq, nk, nv, wq, wk, wv, sink):
    B, N, _ = nq.shape
    blk = lambda n: pl.BlockSpec((1, N, n), lambda b: (b, 0, 0))
    return pl.pallas_call(
        _ctx_attn_kernel,
        grid=(B,),
        in_specs=[pl.BlockSpec(memory_space=pltpu.SMEM),
                  blk(D_NA), blk(D_NA), blk(D_NA), blk(D_WA), blk(D_WA_KV), blk(D_WA_KV)],
        out_specs=[blk(D_NA), blk(D_WA)],
        out_shape=[jax.ShapeDtypeStruct((B, N, D_NA), BF16), jax.ShapeDtypeStruct((B, N, D_WA), BF16)],
        compiler_params=_params(1),
        name="ctx_attention",
    )(sink, nq, nk, nv, wq, wk, wv)


CONV_HALO = 16
SUBLANES = 8


def _conv_kernel(a_ref, ga_ref, w_ref, cb_ref, lg_ref, lb_ref, z_ref, pad_scr, *, tm, n_tiles):
    j = pl.program_id(1)
    t0 = pl.multiple_of(j * tm, tm)
    pad_scr[CONV_HALO:CONV_HALO + tm, :] = a_ref[0, pl.ds(t0, tm), :].astype(F32)
    zeros = jnp.zeros((CONV_HALO, D_CONV), F32)

    @pl.when(j > 0)
    def _():
        pad_scr[0:CONV_HALO, :] = a_ref[0, pl.ds(t0 - CONV_HALO, CONV_HALO), :].astype(F32)

    @pl.when(j == 0)
    def _():
        pad_scr[0:CONV_HALO, :] = zeros

    @pl.when(j < n_tiles - 1)
    def _():
        pad_scr[CONV_HALO + tm:, :] = a_ref[0, pl.ds(t0 + tm, CONV_HALO), :].astype(F32)

    @pl.when(j == n_tiles - 1)
    def _():
        pad_scr[CONV_HALO + tm:, :] = zeros

    acc = jnp.zeros((tm, D_CONV), F32) + cb_ref[...]
    for r in range(SUBLANES):
        shifted = pad_scr[r:r + tm + 3 * SUBLANES, :]
        for jj in range(4):
            k = SUBLANES * jj + r - 1
            if 0 <= k < CONV_K:
                acc = acc + shifted[SUBLANES * jj:SUBLANES * jj + tm] * w_ref[k:k + 1, :]
    mu = jnp.mean(acc, axis=-1, keepdims=True)
    d = acc - mu
    var = jnp.mean(d * d, axis=-1, keepdims=True)
    y = d * lax.rsqrt(var + EPS) * lg_ref[...] + lb_ref[...]
    z_ref[0] = (_silu(y) * ga_ref[0].astype(F32)).astype(BF16)


def _conv_branch(a, ga, conv_w, conv_b, ln_g, ln_b, tm):
    B, T, _ = a.shape
    n_tiles = T // tm
    const2 = lambda b, t: (0, 0)
    row = pl.BlockSpec((1, D_CONV), const2)
    return pl.pallas_call(
        functools.partial(_conv_kernel, tm=tm, n_tiles=n_tiles),
        grid=(B, n_tiles),
        in_specs=[pl.BlockSpec((1, T, D_CONV), lambda b, t: (b, 0, 0)),
                  pl.BlockSpec((1, tm, D_CONV), lambda b, t: (b, t, 0)),
                  pl.BlockSpec((CONV_K, D_CONV), const2),
                  row, row, row],
        out_specs=pl.BlockSpec((1, tm, D_CONV), lambda b, t: (b, t, 0)),
        out_shape=jax.ShapeDtypeStruct((B, T, D_CONV), BF16),
        scratch_shapes=[pltpu.VMEM((tm + 2 * CONV_HALO, D_CONV), F32)],
        compiler_params=_params(2),
        name="conv_branch",
    )(a, ga, conv_w, conv_b, ln_g, ln_b)


def _merge_kernel(z_ref, ona_ref, gn_ref, owa_ref, gw_ref, mg_ref, x_ref, mod_ref,
                  wa_ref, wb_ref, wc_ref, wo_ref, o_ref):
    def gated(o_r, g_r):
        return (o_r[0].astype(F32) * g_r[0].astype(F32)).astype(BF16)

    ya = jnp.dot(z_ref[0], wa_ref[...], preferred_element_type=F32)
    yb = jnp.dot(gated(ona_ref, gn_ref), wb_ref[...], preferred_element_type=F32)
    yc = jnp.dot(gated(owa_ref, gw_ref), wc_ref[...], preferred_element_type=F32)
    y = (mg_ref[0, :, 0:D_MODEL].astype(F32) * ya
         + mg_ref[0, :, D_MODEL:2 * D_MODEL].astype(F32) * yb
         + mg_ref[0, :, 2 * D_MODEL:].astype(F32) * yc)
    gate = mod_ref[0, :, 2 * D_MODEL:]
    o_ref[0] = x_ref[0] + gate * jnp.dot(y.astype(BF16), wo_ref[...], preferred_element_type=F32)


def _merge(z, ona, gn, owa, gw, mg, xs, mod, wa, wb, wc, wo, tm):
    B, T, _ = xs.shape
    tok = lambda n: pl.BlockSpec((1, tm, n), lambda b, t: (b, t, 0))
    const2 = lambda b, t: (0, 0)
    return pl.pallas_call(
        _merge_kernel,
        grid=(B, T // tm),
        in_specs=[tok(D_CONV), tok(D_NA), tok(D_NA), tok(D_WA), tok(D_WA), tok(3 * D_MODEL), tok(D_MODEL),
                  pl.BlockSpec((1, 1, 3 * D_MODEL), lambda b, t: (b, 0, 0)),
                  pl.BlockSpec((D_CONV, D_MODEL), const2),
                  pl.BlockSpec((D_NA, D_MODEL), const2),
                  pl.BlockSpec((D_WA, D_MODEL), const2),
                  pl.BlockSpec((D_MODEL, D_MODEL), const2)],
        out_specs=tok(D_MODEL),
        out_shape=jax.ShapeDtypeStruct((B, T, D_MODEL), F32),
        compiler_params=_params(2),
        name="merge",
    )(z, ona, gn, owa, gw, mg, xs, mod, wa, wb, wc, wo)


def _wa_head_order():
    order = []
    for p in range(WA_GROUP):
        order += [p, WA_GROUP + p]
    return np.asarray(order)


def _wa_column_perm():
    return (_wa_head_order()[:, None] * HEAD_DIM + np.arange(HEAD_DIM)[None, :]).reshape(-1)


def _rope_tables(seq):
    t = np.arange(seq)
    half = HEAD_DIM // 4
    inv = ROPE_BASE ** (-np.arange(half, dtype=np.float32) / half)
    lane = np.arange(LANES)
    d = lane % HEAD_DIM
    use_col = (d // (HEAD_DIM // 2)) == 1
    dd = d % (HEAD_DIM // 2)
    first = dd < half
    pos = jnp.where(use_col[None, :], (t % GRID_W)[:, None], (t // GRID_W)[:, None]).astype(F32)
    ang = pos * jnp.asarray(inv[dd % half], F32)[None, :]
    cos, sin = jnp.cos(ang), jnp.sin(ang)
    s1 = jnp.where(first[None, :], 0.0, sin)
    s2 = jnp.where(first[None, :], -sin, 0.0)
    return cos, s1, s2


def _block_diag_ones():
    i = np.arange(MXU_DIM)
    return jnp.asarray((i[:, None] // HEAD_DIM) == (i[None, :] // HEAD_DIM), BF16)


def _pick_tile(n, target):
    t = min(n, target)
    while n % t:
        t //= 2
    return t


def kernel(x, c, ctx, c_ctx, norm_g, w_ada, b_ada, w_in, conv_w, conv_b, cln_g, cln_b, w_proj_a,
           na_q_norm, na_k_norm, na_rpb, w_proj_b, wa_q_norm, wa_k_norm, wa_sink, w_proj_c, w_o):
    B, S, _ = x.shape
    N = ctx.shape[1]
    depth = w_in.shape[0]
    assert B <= 8
    c_all = jnp.zeros((16, D_MODEL), F32).at[:B].set(c).at[B].set(c_ctx)
    rope_tabs = _rope_tables(S)
    bd = _block_diag_ones()
    perm = _wa_column_perm()
    st = SPLIT_STARTS
    q_scale = HEAD_DIM ** -0.5
    tm_x = _pick_tile(S, 512)
    tm_c = _pick_tile(N, 512)
    tile_row = lambda w, n: jnp.tile(w.astype(F32), n)[None, :]

    for l in range(depth):
        update_ctx = l < depth - 1
        mod = _modulation(c_all, w_ada[l], b_ada[l][None, :])
        mod_x = mod[:B, None, :]
        mod_c = jnp.broadcast_to(mod[B][None, None, :], (B, 1, 3 * D_MODEL))

        w_l = w_in[l]
        w_l = jnp.concatenate([w_l[:, :st[6]], w_l[:, st[6]:st[7]][:, perm], w_l[:, st[7]:st[9]],
                               w_l[:, st[9]:st[10]][:, perm], w_l[:, st[10]:]], axis=1).astype(BF16)
        nqw = tile_row(na_q_norm[l], NA_HEADS) * q_scale
        nkw = tile_row(na_k_norm[l], NA_HEADS)
        wqw = tile_row(wa_q_norm[l], WA_HEADS) * q_scale
        wkw = tile_row(wa_k_norm[l], WA_KV_HEADS)
        g_row = norm_g[l][None, :]

        (a, ga, nq, nk, nv, gn, wq, wk, wv, gw, mg) = _projection(
            x, mod_x, g_row, w_l, nqw, nkw, wqw, wkw, bd, rope_tabs, tm_x)
        (ac, gac, ncq, nkc, nvc, gnc, wcq, wkc, wvc, gwc, mgc) = _projection(
            ctx, mod_c, g_row, w_l, nqw, nkw, wqw, wkw, bd, None, tm_c)

        bias = _na_bias_table(na_rpb[l])
        sink = wa_sink[l].astype(F32)
        o_na = _na_attention(nq, nk, nv, nkc, nvc, bias)
        o_wa = _wa_attention(wq, wk, wv, wkc, wvc, sink)
        z = _conv_branch(a, ga, conv_w[l], conv_b[l][None, :], cln_g[l][None, :], cln_b[l][None, :],
                         _pick_tile(S, 256))
        wa_b = w_proj_a[l].astype(BF16)
        wb_b = w_proj_b[l].astype(BF16)
        wc_b = w_proj_c[l][perm, :].astype(BF16)
        wo_b = w_o[l].astype(BF16)
        x_new = _merge(z, o_na, gn, o_wa, gw, mg, x, mod_x, wa_b, wb_b, wc_b, wo_b, tm_x)

        if update_ctx:
            oc_na, oc_wa = _ctx_attention(ncq, nkc, nvc, wcq, wkc, wvc, sink)
            zc = _conv_branch(ac, gac, conv_w[l], conv_b[l][None, :], cln_g[l][None, :], cln_b[l][None, :],
                              _pick_tile(N, 256))
            ctx = _merge(zc, oc_na, gnc, oc_wa, gwc, mgc, ctx, mod_c, wa_b, wb_b, wc_b, wo_b, tm_c)
        x = x_new
    return x
```

```python
import functools

import numpy as np
import jax
import jax.numpy as jnp
from jax import lax
from jax.experimental import pallas as pl
from jax.experimental.pallas import tpu as pltpu

D_MODEL = 1024
GRID_W = 64
HEAD_DIM = 64
D_CONV = 512
CONV_K = 31
NA_HEADS = 8
NA_MAX_ROWS = 8
NA_COLS = 16
WA_HEADS = 8
WA_KV_HEADS = 2
WA_GROUP = WA_HEADS // WA_KV_HEADS
WA_WINDOW = 128
WA_BLOCK = 128
WA_STACK = 2
ROPE_BASE = 10000.0
EPS = 1e-6
D_NA = NA_HEADS * HEAD_DIM
D_WA = WA_HEADS * HEAD_DIM
D_WA_KV = WA_KV_HEADS * HEAD_DIM
SPLIT_SIZES = (2 * D_CONV, D_CONV, D_NA, D_NA, D_NA, D_NA, D_WA, D_WA_KV, D_WA_KV, D_WA, 3 * D_MODEL)
D_IN = sum(SPLIT_SIZES)
SPLIT_STARTS = tuple(int(v) for v in np.cumsum((0,) + SPLIT_SIZES[:-1]))

LANES = 128
MXU_DIM = 256
HEAD_PAIR = 2 * HEAD_DIM
ROPE_HALF = HEAD_DIM // 4
NEG_BIG = -1e30
LOG2E = 1.4426950408889634
VMEM_LIMIT = 56 * 1024 * 1024

F32 = jnp.float32
BF16 = jnp.bfloat16


def _sigmoid(v):
    return 1.0 / (1.0 + jnp.exp(-v))


def _silu(v):
    return v * _sigmoid(v)


def _params(n_grid):
    return pltpu.CompilerParams(dimension_semantics=("arbitrary",) * n_grid,
                                vmem_limit_bytes=VMEM_LIMIT)


def _mod_kernel(c_ref, w_ref, b_ref, o_ref):
    s = _silu(c_ref[...])
    o_ref[...] = jnp.dot(s.astype(BF16), w_ref[...].astype(BF16),
                         preferred_element_type=F32) + b_ref[...]


def _modulation(c_all, w_ada, b_ada):
    rows = c_all.shape[0]
    n_out = w_ada.shape[1]
    blk = D_MODEL
    return pl.pallas_call(
        _mod_kernel,
        grid=(n_out // blk,),
        in_specs=[pl.BlockSpec((rows, D_MODEL), lambda j: (0, 0)),
                  pl.BlockSpec((D_MODEL, blk), lambda j: (0, j)),
                  pl.BlockSpec((1, blk), lambda j: (0, j))],
        out_specs=pl.BlockSpec((rows, blk), lambda j: (0, j)),
        out_shape=jax.ShapeDtypeStruct((rows, n_out), F32),
        compiler_params=_params(1),
        name="modulation",
    )(c_all, w_ada, b_ada)


def _proj_kernel(*refs, rope):
    if rope:
        (x_ref, mod_ref, g_ref, w_ref, nqw_ref, nkw_ref, wqw_ref, wkw_ref, bd_ref,
         cos_ref, s1_ref, s2_ref) = refs[:12]
        outs = refs[12:]
    else:
        (x_ref, mod_ref, g_ref, w_ref, nqw_ref, nkw_ref, wqw_ref, wkw_ref, bd_ref) = refs[:9]
        cos_ref = s1_ref = s2_ref = None
        outs = refs[9:]
    (a_ref, ga_ref, nq_ref, nk_ref, nv_ref, gn_ref, wq_ref, wk_ref, wv_ref, gw_ref, mg_ref,
     h_scr) = outs

    x = x_ref[0]
    ms = jnp.mean(x * x, axis=-1, keepdims=True)
    shift = mod_ref[0, :, 0:D_MODEL]
    scale = mod_ref[0, :, D_MODEL:2 * D_MODEL]
    h = (x * lax.rsqrt(ms + EPS) * g_ref[...]) * (1.0 + scale) + shift
    h_scr[...] = h.astype(BF16)

    st = SPLIT_STARTS

    def seg(c0, n):
        return jnp.dot(h_scr[...], w_ref[:, c0:c0 + n], preferred_element_type=F32)

    def head_norm(t, w_row_ref):
        n = t.shape[1]
        sq = (t * t).astype(BF16)
        if n >= MXU_DIM:
            parts = [jnp.dot(sq[:, c:c + MXU_DIM], bd_ref[...], preferred_element_type=F32)
                     for c in range(0, n, MXU_DIM)]
            ss = jnp.concatenate(parts, axis=1) if len(parts) > 1 else parts[0]
        else:
            ss = jnp.dot(sq, bd_ref[0:n, 0:n], preferred_element_type=F32)
        return t * lax.rsqrt(ss * (1.0 / HEAD_DIM) + EPS) * w_row_ref[...]

    def store_rot(t, o_ref):
        n = t.shape[1]
        for c in range(0, n, LANES):
            tc = t[:, c:c + LANES]
            if rope:
                lo = pltpu.roll(tc, ROPE_HALF, axis=1)
                hi = pltpu.roll(tc, LANES - ROPE_HALF, axis=1)
                tc = tc * cos_ref[...] + lo * s1_ref[...] + hi * s2_ref[...]
            o_ref[0, :, c:c + LANES] = tc.astype(BF16)

    glu = seg(st[0], 2 * D_CONV)
    a_ref[0] = (glu[:, :D_CONV] * _sigmoid(glu[:, D_CONV:])).astype(BF16)
    ga_ref[0] = _silu(seg(st[1], D_CONV)).astype(BF16)
    nq_ref[0] = head_norm(seg(st[2], D_NA), nqw_ref).astype(BF16)
    nk_ref[0] = head_norm(seg(st[3], D_NA), nkw_ref).astype(BF16)
    nv_ref[0] = seg(st[4], D_NA).astype(BF16)
    gn_ref[0] = _silu(seg(st[5], D_NA)).astype(BF16)
    store_rot(head_norm(seg(st[6], D_WA), wqw_ref), wq_ref)
    store_rot(head_norm(seg(st[7], D_WA_KV), wkw_ref), wk_ref)
    wv_ref[0] = seg(st[8], D_WA_KV).astype(BF16)
    gw_ref[0] = _silu(seg(st[9], D_WA)).astype(BF16)
    for c in range(0, 3 * D_MODEL, D_CONV):
        mg_ref[0, :, c:c + D_CONV] = _sigmoid(seg(st[10] + c, D_CONV)).astype(BF16)


_PROJ_OUT_WIDTHS = (D_CONV, D_CONV, D_NA, D_NA, D_NA, D_NA, D_WA, D_WA_KV, D_WA_KV, D_WA, 3 * D_MODEL)


def _projection(xs, mod, norm_g, w_in, nqw, nkw, wqw, wkw, bd, rope_tabs, tm):
    B, T, _ = xs.shape
    rope = rope_tabs is not None
    const2 = lambda b, t: (0, 0)
    row_spec = lambda n: pl.BlockSpec((1, n), const2)
    in_specs = [
        pl.BlockSpec((1, tm, D_MODEL), lambda b, t: (b, t, 0)),
        pl.BlockSpec((1, 1, 3 * D_MODEL), lambda b, t: (b, 0, 0)),
        row_spec(D_MODEL),
        pl.BlockSpec((D_MODEL, D_IN), const2, pipeline_mode=pl.Buffered(1)),
        row_spec(D_NA), row_spec(D_NA), row_spec(D_WA), row_spec(D_WA_KV),
        pl.BlockSpec((MXU_DIM, MXU_DIM), const2),
    ]
    args = [xs, mod, norm_g, w_in, nqw, nkw, wqw, wkw, bd]
    if rope:
        in_specs += [pl.BlockSpec((tm, LANES), lambda b, t: (t, 0))] * 3
        args += list(rope_tabs)
    out_specs = [pl.BlockSpec((1, tm, n), lambda b, t: (b, t, 0)) for n in _PROJ_OUT_WIDTHS]
    out_shape = [jax.ShapeDtypeStruct((B, T, n), BF16) for n in _PROJ_OUT_WIDTHS]
    return pl.pallas_call(
        functools.partial(_proj_kernel, rope=rope),
        grid=(B, T // tm),
        in_specs=in_specs,
        out_specs=out_specs,
        out_shape=out_shape,
        scratch_shapes=[pltpu.VMEM((tm, D_MODEL), BF16)],
        compiler_params=_params(2),
        name="projection_rope" if rope else "projection_ctx",
    )(*args)


def _nt_dot(a, b):
    return lax.dot_general(a, b, (((1,), (1,)), ((), ())), preferred_element_type=F32)


def _low_half(rows):
    return lax.broadcasted_iota(jnp.int32, (rows, HEAD_PAIR), 1) < HEAD_DIM


def _keep_half(q_pair, half):
    lane = lax.broadcasted_iota(jnp.int32, (1, HEAD_PAIR), 1)
    keep = (lane < HEAD_DIM) if half == 0 else (lane >= HEAD_DIM)
    return q_pair * keep.astype(F32).astype(q_pair.dtype)


def _softmax_pv(scores, values, sinks=None):
    rows = scores.shape[0]
    n_chunks = len(sinks) if sinks else 1
    chunk = rows // n_chunks
    p_chunks, inv_chunks = [], []
    for c in range(n_chunks):
        s = scores[c * chunk:(c + 1) * chunk]
        m = jnp.max(s, axis=-1, keepdims=True)
        if sinks:
            m = jnp.maximum(m, sinks[c])
        p = jnp.exp2(s - m)
        den = jnp.sum(p, axis=-1, keepdims=True)
        if sinks:
            den = den + jnp.exp2(sinks[c] - m)
        p_chunks.append(p.astype(BF16))
        inv_chunks.append(1.0 / den)
    p = p_chunks[0] if n_chunks == 1 else jnp.concatenate(p_chunks, axis=0)
    inv = inv_chunks[0] if n_chunks == 1 else jnp.concatenate(inv_chunks, axis=0)
    return jnp.dot(p, values, preferred_element_type=F32), inv


def _pipelined(problems, scores_fn, finish_fn):
    outs = {}
    pending = scores_fn(*problems[0])
    for i, prob in enumerate(problems):
        nxt = scores_fn(*problems[i + 1]) if i + 1 < len(problems) else None
        outs[prob] = finish_fn(*prob, pending)
        pending = nxt
    return outs


NA_QROWS = 4
NA_KROWS = NA_QROWS + NA_MAX_ROWS


def _na_kernel(q_ref, k_ref, v_ref, kc_ref, vc_ref, bias_ref, o_ref, *, n_rows):
    t = pl.program_id(1)
    tq = NA_QROWS * GRID_W
    n_win = NA_KROWS * GRID_W
    kb = jnp.clip(t * NA_QROWS - NA_MAX_ROWS // 2, 0, n_rows - NA_KROWS)
    k0 = pl.multiple_of(kb * GRID_W, GRID_W)
    low = _low_half(tq)

    def scores(hp):
        lanes = slice(hp * HEAD_PAIR, (hp + 1) * HEAD_PAIR)
        q_pair = q_ref[0, :, lanes]
        q_stack = jnp.concatenate([_keep_half(q_pair, 0), _keep_half(q_pair, 1)], axis=0)
        keys = jnp.concatenate([k_ref[0, pl.ds(k0, n_win), lanes], kc_ref[0, :, lanes]], axis=0)
        s = _nt_dot(q_stack, keys)
        return jnp.concatenate([s[:, :n_win] + bias_ref[0, hp], s[:, n_win:]], axis=1)

    def finish(hp, sc):
        lanes = slice(hp * HEAD_PAIR, (hp + 1) * HEAD_PAIR)
        vals = jnp.concatenate([v_ref[0, pl.ds(k0, n_win), lanes], vc_ref[0, :, lanes]], axis=0)
        o, inv = _softmax_pv(sc, vals)
        o = o * inv
        o_ref[0, :, lanes] = jnp.where(low, o[:tq], o[tq:]).astype(BF16)
        return None

    _pipelined([(hp,) for hp in range(NA_HEADS // 2)], scores, finish)


def _na_attention(q, k, v, kc, vc, bias):
    B, S, _ = q.shape
    N = kc.shape[1]
    n_rows = S // GRID_W
    assert S % GRID_W == 0 and n_rows % NA_QROWS == 0 and n_rows >= NA_KROWS
    n_steps = n_rows // NA_QROWS
    tq = NA_QROWS * GRID_W
    full = lambda b, t: (b, 0, 0)
    variant = lambda b, t: (jnp.where(t == 0, 0, jnp.where(t == n_steps - 1, 2, 1)), 0, 0, 0)
    return pl.pallas_call(
        functools.partial(_na_kernel, n_rows=n_rows),
        grid=(B, n_steps),
        in_specs=[pl.BlockSpec((1, tq, D_NA), lambda b, t: (b, t, 0)),
                  pl.BlockSpec((1, S, D_NA), full),
                  pl.BlockSpec((1, S, D_NA), full),
                  pl.BlockSpec((1, N, D_NA), full),
                  pl.BlockSpec((1, N, D_NA), full),
                  pl.BlockSpec((1,) + bias.shape[1:], variant)],
        out_specs=pl.BlockSpec((1, tq, D_NA), lambda b, t: (b, t, 0)),
        out_shape=jax.ShapeDtypeStruct((B, S, D_NA), BF16),
        compiler_params=_params(2),
        name="na_attention",
    )(q, k, v, kc, vc, bias)


def _na_bias_table(rpb, n_rows):
    j = np.arange(GRID_W)[:, None]
    kc = np.arange(GRID_W)[None, :]
    cs = np.clip(j - NA_COLS // 2, 0, GRID_W - NA_COLS)
    col_ok = (kc >= cs) & (kc < cs + NA_COLS)
    c_off = np.clip(kc - j + (NA_COLS - 1), 0, 2 * NA_COLS - 2)
    tabs = []
    for i0 in (0, NA_QROWS, n_rows - NA_QROWS):
        kb = int(np.clip(i0 - NA_MAX_ROWS // 2, 0, n_rows - NA_KROWS))
        i = i0 + np.arange(NA_QROWS)[:, None]
        a = kb + np.arange(NA_KROWS)[None, :]
        rs = np.clip(i - NA_MAX_ROWS // 2, 0, n_rows - NA_MAX_ROWS)
        row_ok = (a >= rs) & (a < rs + NA_MAX_ROWS)
        r_off = np.clip(a - i + (NA_MAX_ROWS - 1), 0, 2 * NA_MAX_ROWS - 2)
        vals = rpb.astype(F32)[:, r_off[:, None, :, None], c_off[None, :, None, :]]
        ok = row_ok[:, None, :, None] & col_ok[None, :, None, :]
        tab = jnp.where(ok[None], vals * LOG2E, NEG_BIG).reshape(
            NA_HEADS // 2, 2 * NA_QROWS * GRID_W, NA_KROWS * GRID_W)
        tabs.append(tab)
    return jnp.stack(tabs)


def _wa_kernel(sink_ref, q_ref, k_ref, v_ref, kc_ref, vc_ref, o_ref, *, seq):
    n = pl.program_id(1)
    span = WA_BLOCK + 2 * WA_WINDOW
    ks = jnp.clip(n * WA_BLOCK - WA_WINDOW, 0, seq - span)
    k0 = pl.multiple_of(ks, WA_BLOCK)
    keys = jnp.concatenate([k_ref[0, pl.ds(k0, span), :], kc_ref[0]], axis=0)
    vals = jnp.concatenate([v_ref[0, pl.ds(k0, span), :], vc_ref[0]], axis=0)
    row = lax.broadcasted_iota(jnp.int32, (WA_BLOCK, span), 0)
    col = lax.broadcasted_iota(jnp.int32, (WA_BLOCK, span), 1)
    rel = col + (ks - n * WA_BLOCK) - row
    mask_add = jnp.where(jnp.abs(rel) <= WA_WINDOW, 0.0, NEG_BIG)
    mask_add = jnp.concatenate([mask_add] * WA_STACK, axis=0)
    low = _low_half(WA_BLOCK)
    problems = [(kh, gs) for kh in range(WA_KV_HEADS) for gs in range(0, WA_GROUP, WA_STACK)]

    def scores(kh, gs):
        q_stack = jnp.concatenate(
            [_keep_half(q_ref[0, :, g * HEAD_PAIR:(g + 1) * HEAD_PAIR], kh) for g in range(gs, gs + WA_STACK)],
            axis=0)
        s = _nt_dot(q_stack, keys)
        return jnp.concatenate([s[:, :span] + mask_add, s[:, span:]], axis=1)

    def finish(kh, gs, sc):
        sinks = [sink_ref[kh * WA_GROUP + g] * LOG2E for g in range(gs, gs + WA_STACK)]
        o, inv = _softmax_pv(sc, vals, sinks)
        return o * inv

    outs = _pipelined(problems, scores, finish)
    for g in range(WA_GROUP):
        gs, r0 = (g // WA_STACK) * WA_STACK, (g % WA_STACK) * WA_BLOCK
        lo = outs[(0, gs)][r0:r0 + WA_BLOCK]
        hi = outs[(1, gs)][r0:r0 + WA_BLOCK]
        o_ref[0, :, g * HEAD_PAIR:(g + 1) * HEAD_PAIR] = jnp.where(low, lo, hi).astype(BF16)


def _wa_attention(q, k, v, kc, vc, sink):
    B, S, _ = q.shape
    N = kc.shape[1]
    assert S % WA_BLOCK == 0 and S >= WA_BLOCK + 2 * WA_WINDOW
    full = lambda b, t: (b, 0, 0)
    return pl.pallas_call(
        functools.partial(_wa_kernel, seq=S),
        grid=(B, S // WA_BLOCK),
        in_specs=[pl.BlockSpec(memory_space=pltpu.SMEM),
                  pl.BlockSpec((1, WA_BLOCK, D_WA), lambda b, t: (b, t, 0)),
                  pl.BlockSpec((1, S, D_WA_KV), full),
                  pl.BlockSpec((1, S, D_WA_KV), full),
                  pl.BlockSpec((1, N, D_WA_KV), full),
                  pl.BlockSpec((1, N, D_WA_KV), full)],
        out_specs=pl.BlockSpec((1, WA_BLOCK, D_WA), lambda b, t: (b, t, 0)),
        out_shape=jax.ShapeDtypeStruct((B, S, D_WA), BF16),
        compiler_params=_params(2),
        name="wa_attention",
    )(sink, q, k, v, kc, vc)


def _ctx_attn_kernel(sink_ref, nq_ref, nk_ref, nv_ref, wq_ref, wk_ref, wv_ref, ona_ref, owa_ref):
    n_ctx = nq_ref.shape[1]
    low = _low_half(n_ctx)
    for hp in range(NA_HEADS // 2):
        lanes = slice(hp * HEAD_PAIR, (hp + 1) * HEAD_PAIR)
        q_pair = nq_ref[0, :, lanes]
        k_c = nk_ref[0, :, lanes]
        v_c = nv_ref[0, :, lanes]
        halves = []
        for half in range(2):
            qh = _keep_half(q_pair, half)
            o, inv = _softmax_pv(_nt_dot(qh, k_c), v_c)
            halves.append(o * inv)
        ona_ref[0, :, lanes] = jnp.where(low, halves[0], halves[1]).astype(BF16)
    k_c = wk_ref[0]
    v_c = wv_ref[0]
    for g in range(WA_GROUP):
        lanes = slice(g * HEAD_PAIR, (g + 1) * HEAD_PAIR)
        q_pair = wq_ref[0, :, lanes]
        halves = []
        for kh in range(WA_KV_HEADS):
            qh = _keep_half(q_pair, kh)
            o, inv = _softmax_pv(_nt_dot(qh, k_c), v_c, [sink_ref[kh * WA_GROUP + g] * LOG2E])
            halves.append(o * inv)
        owa_ref[0, :, lanes] = jnp.where(low, halves[0], halves[1]).astype(BF16)


def _ctx_attention(nq, nk, nv, wq, wk, wv, sink):
    B, N, _ = nq.shape
    blk = lambda n: pl.BlockSpec((1, N, n), lambda b: (b, 0, 0))
    return pl.pallas_call(
        _ctx_attn_kernel,
        grid=(B,),
        in_specs=[pl.BlockSpec(memory_space=pltpu.SMEM),
                  blk(D_NA), blk(D_NA), blk(D_NA), blk(D_WA), blk(D_WA_KV), blk(D_WA_KV)],
        out_specs=[blk(D_NA), blk(D_WA)],
        out_shape=[jax.ShapeDtypeStruct((B, N, D_NA), BF16), jax.ShapeDtypeStruct((B, N, D_WA), BF16)],
        compiler_params=_params(1),
        name="ctx_attention",
    )(sink, nq, nk, nv, wq, wk, wv)


CONV_HALO = 16
SUBLANES = 8


def _conv_kernel(a_ref, ga_ref, w_ref, cb_ref, lg_ref, lb_ref, z_ref, pad_scr, *, tm, n_tiles):
    j = pl.program_id(1)
    t0 = pl.multiple_of(j * tm, tm)
    pad_scr[CONV_HALO:CONV_HALO + tm, :] = a_ref[0, pl.ds(t0, tm), :].astype(F32)
    zeros = jnp.zeros((CONV_HALO, D_CONV), F32)

    @pl.when(j > 0)
    def _():
        pad_scr[0:CONV_HALO, :] = a_ref[0, pl.ds(t0 - CONV_HALO, CONV_HALO), :].astype(F32)

    @pl.when(j == 0)
    def _():
        pad_scr[0:CONV_HALO, :] = zeros

    @pl.when(j < n_tiles - 1)
    def _():
        pad_scr[CONV_HALO + tm:, :] = a_ref[0, pl.ds(t0 + tm, CONV_HALO), :].astype(F32)

    @pl.when(j == n_tiles - 1)
    def _():
        pad_scr[CONV_HALO + tm:, :] = zeros

    acc = jnp.zeros((tm, D_CONV), F32) + cb_ref[...]
    for r in range(SUBLANES):
        shifted = pad_scr[r:r + tm + 3 * SUBLANES, :]
        for jj in range(4):
            k = SUBLANES * jj + r - 1
            if 0 <= k < CONV_K:
                acc = acc + shifted[SUBLANES * jj:SUBLANES * jj + tm] * w_ref[k:k + 1, :]
    mu = jnp.mean(acc, axis=-1, keepdims=True)
    d = acc - mu
    var = jnp.mean(d * d, axis=-1, keepdims=True)
    y = d * lax.rsqrt(var + EPS) * lg_ref[...] + lb_ref[...]
    z_ref[0] = (_silu(y) * ga_ref[0].astype(F32)).astype(BF16)


def _conv_branch(a, ga, conv_w, conv_b, ln_g, ln_b, tm):
    B, T, _ = a.shape
    n_tiles = T // tm
    const2 = lambda b, t: (0, 0)
    row = pl.BlockSpec((1, D_CONV), const2)
    return pl.pallas_call(
        functools.partial(_conv_kernel, tm=tm, n_tiles=n_tiles),
        grid=(B, n_tiles),
        in_specs=[pl.BlockSpec((1, T, D_CONV), lambda b, t: (b, 0, 0)),
                  pl.BlockSpec((1, tm, D_CONV), lambda b, t: (b, t, 0)),
                  pl.BlockSpec((CONV_K, D_CONV), const2),
                  row, row, row],
        out_specs=pl.BlockSpec((1, tm, D_CONV), lambda b, t: (b, t, 0)),
        out_shape=jax.ShapeDtypeStruct((B, T, D_CONV), BF16),
        scratch_shapes=[pltpu.VMEM((tm + 2 * CONV_HALO, D_CONV), F32)],
        compiler_params=_params(2),
        name="conv_branch",
    )(a, ga, conv_w, conv_b, ln_g, ln_b)


def _merge_kernel(z_ref, ona_ref, gn_ref, owa_ref, gw_ref, mg_ref, x_ref, mod_ref,
                  wa_ref, wb_ref, wc_ref, wo_ref, o_ref):
    def gated(o_r, g_r):
        return (o_r[0].astype(F32) * g_r[0].astype(F32)).astype(BF16)

    ya = jnp.dot(z_ref[0], wa_ref[...], preferred_element_type=F32)
    yb = jnp.dot(gated(ona_ref, gn_ref), wb_ref[...], preferred_element_type=F32)
    yc = jnp.dot(gated(owa_ref, gw_ref), wc_ref[...], preferred_element_type=F32)
    y = (mg_ref[0, :, 0:D_MODEL].astype(F32) * ya
         + mg_ref[0, :, D_MODEL:2 * D_MODEL].astype(F32) * yb
         + mg_ref[0, :, 2 * D_MODEL:].astype(F32) * yc)
    gate = mod_ref[0, :, 2 * D_MODEL:]
    o_ref[0] = x_ref[0] + gate * jnp.dot(y.astype(BF16), wo_ref[...], preferred_element_type=F32)


def _merge(z, ona, gn, owa, gw, mg, xs, mod, wa, wb, wc, wo, tm):
    B, T, _ = xs.shape
    tok = lambda n: pl.BlockSpec((1, tm, n), lambda b, t: (b, t, 0))
    const2 = lambda b, t: (0, 0)
    return pl.pallas_call(
        _merge_kernel,
        grid=(B, T // tm),
        in_specs=[tok(D_CONV), tok(D_NA), tok(D_NA), tok(D_WA), tok(D_WA), tok(3 * D_MODEL), tok(D_MODEL),
                  pl.BlockSpec((1, 1, 3 * D_MODEL), lambda b, t: (b, 0, 0)),
                  pl.BlockSpec((D_CONV, D_MODEL), const2),
                  pl.BlockSpec((D_NA, D_MODEL), const2),
                  pl.BlockSpec((D_WA, D_MODEL), const2),
                  pl.BlockSpec((D_MODEL, D_MODEL), const2)],
        out_specs=tok(D_MODEL),
        out_shape=jax.ShapeDtypeStruct((B, T, D_MODEL), F32),
        compiler_params=_params(2),
        name="merge",
    )(z, ona, gn, owa, gw, mg, xs, mod, wa, wb, wc, wo)


def _wa_head_order():
    order = []
    for p in range(WA_GROUP):
        order += [p, WA_GROUP + p]
    return np.asarray(order)


def _wa_column_perm():
    return (_wa_head_order()[:, None] * HEAD_DIM + np.arange(HEAD_DIM)[None, :]).reshape(-1)


def _rope_tables(seq):
    t = np.arange(seq)
    half = HEAD_DIM // 4
    inv = ROPE_BASE ** (-np.arange(half, dtype=np.float32) / half)
    lane = np.arange(LANES)
    d = lane % HEAD_DIM
    use_col = (d // (HEAD_DIM // 2)) == 1
    dd = d % (HEAD_DIM // 2)
    first = dd < half
    pos = jnp.where(use_col[None, :], (t % GRID_W)[:, None], (t // GRID_W)[:, None]).astype(F32)
    ang = pos * jnp.asarray(inv[dd % half], F32)[None, :]
    cos, sin = jnp.cos(ang), jnp.sin(ang)
    s1 = jnp.where(first[None, :], 0.0, sin)
    s2 = jnp.where(first[None, :], -sin, 0.0)
    return cos, s1, s2


def _block_diag_ones():
    i = np.arange(MXU_DIM)
    return jnp.asarray((i[:, None] // HEAD_DIM) == (i[None, :] // HEAD_DIM), BF16)


def _pick_tile(n, target):
    t = min(n, target)
    while n % t:
        t //= 2
    return t


def kernel(x, c, ctx, c_ctx, norm_g, w_ada, b_ada, w_in, conv_w, conv_b, cln_g, cln_b, w_proj_a,
           na_q_norm, na_k_norm, na_rpb, w_proj_b, wa_q_norm, wa_k_norm, wa_sink, w_proj_c, w_o):
    B, S, _ = x.shape
    N = ctx.shape[1]
    depth = w_in.shape[0]
    assert B <= 8
    c_all = jnp.zeros((16, D_MODEL), F32).at[:B].set(c).at[B].set(c_ctx)
    rope_tabs = _rope_tables(S)
    bd = _block_diag_ones()
    perm = _wa_column_perm()
    st = SPLIT_STARTS
    q_scale = HEAD_DIM ** -0.5 * LOG2E
    tm_x = _pick_tile(S, 512)
    tm_c = _pick_tile(N, 512)
    tile_row = lambda w, n: jnp.tile(w.astype(F32), n)[None, :]

    for l in range(depth):
        update_ctx = l < depth - 1
        mod = _modulation(c_all, w_ada[l], b_ada[l][None, :])
        mod_x = mod[:B, None, :]
        mod_c = jnp.broadcast_to(mod[B][None, None, :], (B, 1, 3 * D_MODEL))

        w_l = w_in[l]
        w_l = jnp.concatenate([w_l[:, :st[6]], w_l[:, st[6]:st[7]][:, perm], w_l[:, st[7]:st[9]],
                               w_l[:, st[9]:st[10]][:, perm], w_l[:, st[10]:]], axis=1).astype(BF16)
        nqw = tile_row(na_q_norm[l], NA_HEADS) * q_scale
        nkw = tile_row(na_k_norm[l], NA_HEADS)
        wqw = tile_row(wa_q_norm[l], WA_HEADS) * q_scale
        wkw = tile_row(wa_k_norm[l], WA_KV_HEADS)
        g_row = norm_g[l][None, :]

        (a, ga, nq, nk, nv, gn, wq, wk, wv, gw, mg) = _projection(
            x, mod_x, g_row, w_l, nqw, nkw, wqw, wkw, bd, rope_tabs, tm_x)
        (ac, gac, ncq, nkc, nvc, gnc, wcq, wkc, wvc, gwc, mgc) = _projection(
            ctx, mod_c, g_row, w_l, nqw, nkw, wqw, wkw, bd, None, tm_c)

        bias = _na_bias_table(na_rpb[l], S // GRID_W)
        sink = wa_sink[l].astype(F32)
        o_na = _na_attention(nq, nk, nv, nkc, nvc, bias)
        o_wa = _wa_attention(wq, wk, wv, wkc, wvc, sink)
        z = _conv_branch(a, ga, conv_w[l], conv_b[l][None, :], cln_g[l][None, :], cln_b[l][None, :],
                         _pick_tile(S, 256))
        wa_b = w_proj_a[l].astype(BF16)
        wb_b = w_proj_b[l].astype(BF16)
        wc_b = w_proj_c[l][perm, :].astype(BF16)
        wo_b = w_o[l].astype(BF16)
        x_new = _merge(z, o_na, gn, o_wa, gw, mg, x, mod_x, wa_b, wb_b, wc_b, wo_b, tm_x)

        if update_ctx:
            oc_na, oc_wa = _ctx_attention(ncq, nkc, nvc, wcq, wkc, wvc, sink)
            zc = _conv_branch(ac, gac, conv_w[l], conv_b[l][None, :], cln_g[l][None, :], cln_b[l][None, :],
                              _pick_tile(N, 256))
            ctx = _merge(zc, oc_na, gnc, oc_wa, gwc, mgc, ctx, mod_c, wa_b, wb_b, wc_b, wo_b, tm_c)
        x = x_new
    return x
```

```python
import functools

import numpy as np
import jax
import jax.numpy as jnp
from jax import lax
from jax.experimental import pallas as pl
from jax.experimental.pallas import tpu as pltpu

D_MODEL = 1024
GRID_W = 64
HEAD_DIM = 64
D_CONV = 512
CONV_K = 31
NA_HEADS = 8
NA_MAX_ROWS = 8
NA_COLS = 16
WA_HEADS = 8
WA_KV_HEADS = 2
WA_GROUP = WA_HEADS // WA_KV_HEADS
WA_WINDOW = 128
WA_BLOCK = 128
WA_STACK = 2
ROPE_BASE = 10000.0
EPS = 1e-6
D_NA = NA_HEADS * HEAD_DIM
D_WA = WA_HEADS * HEAD_DIM
D_WA_KV = WA_KV_HEADS * HEAD_DIM
SPLIT_SIZES = (2 * D_CONV, D_CONV, D_NA, D_NA, D_NA, D_NA, D_WA, D_WA_KV, D_WA_KV, D_WA, 3 * D_MODEL)
D_IN = sum(SPLIT_SIZES)
SPLIT_STARTS = tuple(int(v) for v in np.cumsum((0,) + SPLIT_SIZES[:-1]))

LANES = 128
MXU_DIM = 256
HEAD_PAIR = 2 * HEAD_DIM
ROPE_HALF = HEAD_DIM // 4
NEG_BIG = -1e30
LOG2E = 1.4426950408889634
VMEM_LIMIT = 56 * 1024 * 1024

F32 = jnp.float32
BF16 = jnp.bfloat16


def _sigmoid(v):
    return 1.0 / (1.0 + jnp.exp(-v))


def _silu(v):
    return v * _sigmoid(v)


def _params(n_grid):
    return pltpu.CompilerParams(dimension_semantics=("arbitrary",) * n_grid,
                                vmem_limit_bytes=VMEM_LIMIT)


def _mod_kernel(c_ref, w_ref, b_ref, o_ref):
    s = _silu(c_ref[...])
    o_ref[...] = jnp.dot(s.astype(BF16), w_ref[...].astype(BF16),
                         preferred_element_type=F32) + b_ref[...]


def _modulation(c_all, w_ada, b_ada):
    rows = c_all.shape[0]
    n_out = w_ada.shape[1]
    blk = D_MODEL
    return pl.pallas_call(
        _mod_kernel,
        grid=(n_out // blk,),
        in_specs=[pl.BlockSpec((rows, D_MODEL), lambda j: (0, 0)),
                  pl.BlockSpec((D_MODEL, blk), lambda j: (0, j)),
                  pl.BlockSpec((1, blk), lambda j: (0, j))],
        out_specs=pl.BlockSpec((rows, blk), lambda j: (0, j)),
        out_shape=jax.ShapeDtypeStruct((rows, n_out), F32),
        compiler_params=_params(1),
        name="modulation",
    )(c_all, w_ada, b_ada)


def _proj_kernel(*refs, rope):
    if rope:
        (x_ref, mod_ref, g_ref, w_ref, nqw_ref, nkw_ref, wqw_ref, wkw_ref, bd_ref,
         cos_ref, s1_ref, s2_ref) = refs[:12]
        outs = refs[12:]
    else:
        (x_ref, mod_ref, g_ref, w_ref, nqw_ref, nkw_ref, wqw_ref, wkw_ref, bd_ref) = refs[:9]
        cos_ref = s1_ref = s2_ref = None
        outs = refs[9:]
    (a_ref, ga_ref, nq_ref, nk_ref, nv_ref, gn_ref, wq_ref, wk_ref, wv_ref, gw_ref, mg_ref,
     h_scr) = outs

    x = x_ref[0]
    ms = jnp.mean(x * x, axis=-1, keepdims=True)
    shift = mod_ref[0, :, 0:D_MODEL]
    scale = mod_ref[0, :, D_MODEL:2 * D_MODEL]
    h = (x * lax.rsqrt(ms + EPS) * g_ref[...]) * (1.0 + scale) + shift
    h_scr[...] = h.astype(BF16)

    st = SPLIT_STARTS

    def seg(c0, n):
        return jnp.dot(h_scr[...], w_ref[:, c0:c0 + n], preferred_element_type=F32)

    def head_norm(t, w_row_ref):
        n = t.shape[1]
        sq = (t * t).astype(BF16)
        if n >= MXU_DIM:
            parts = [jnp.dot(sq[:, c:c + MXU_DIM], bd_ref[...], preferred_element_type=F32)
                     for c in range(0, n, MXU_DIM)]
            ss = jnp.concatenate(parts, axis=1) if len(parts) > 1 else parts[0]
        else:
            ss = jnp.dot(sq, bd_ref[0:n, 0:n], preferred_element_type=F32)
        return t * lax.rsqrt(ss * (1.0 / HEAD_DIM) + EPS) * w_row_ref[...]

    def store_rot(t, o_ref):
        n = t.shape[1]
        for c in range(0, n, LANES):
            tc = t[:, c:c + LANES]
            if rope:
                lo = pltpu.roll(tc, ROPE_HALF, axis=1)
                hi = pltpu.roll(tc, LANES - ROPE_HALF, axis=1)
                tc = tc * cos_ref[...] + lo * s1_ref[...] + hi * s2_ref[...]
            o_ref[0, :, c:c + LANES] = tc.astype(BF16)

    glu = seg(st[0], 2 * D_CONV)
    a_ref[0] = (glu[:, :D_CONV] * _sigmoid(glu[:, D_CONV:])).astype(BF16)
    ga_ref[0] = _silu(seg(st[1], D_CONV)).astype(BF16)
    nq_ref[0] = head_norm(seg(st[2], D_NA), nqw_ref).astype(BF16)
    nk_ref[0] = head_norm(seg(st[3], D_NA), nkw_ref).astype(BF16)
    nv_ref[0] = seg(st[4], D_NA).astype(BF16)
    gn_ref[0] = _silu(seg(st[5], D_NA)).astype(BF16)
    store_rot(head_norm(seg(st[6], D_WA), wqw_ref), wq_ref)
    store_rot(head_norm(seg(st[7], D_WA_KV), wkw_ref), wk_ref)
    wv_ref[0] = seg(st[8], D_WA_KV).astype(BF16)
    gw_ref[0] = _silu(seg(st[9], D_WA)).astype(BF16)
    for c in range(0, 3 * D_MODEL, D_CONV):
        mg_ref[0, :, c:c + D_CONV] = _sigmoid(seg(st[10] + c, D_CONV)).astype(BF16)


_PROJ_OUT_WIDTHS = (D_CONV, D_CONV, D_NA, D_NA, D_NA, D_NA, D_WA, D_WA_KV, D_WA_KV, D_WA, 3 * D_MODEL)


def _projection(xs, mod, norm_g, w_in, nqw, nkw, wqw, wkw, bd, rope_tabs, tm):
    B, T, _ = xs.shape
    rope = rope_tabs is not None
    const2 = lambda b, t: (0, 0)
    row_spec = lambda n: pl.BlockSpec((1, n), const2)
    in_specs = [
        pl.BlockSpec((1, tm, D_MODEL), lambda b, t: (b, t, 0)),
        pl.BlockSpec((1, 1, 3 * D_MODEL), lambda b, t: (b, 0, 0)),
        row_spec(D_MODEL),
        pl.BlockSpec((D_MODEL, D_IN), const2, pipeline_mode=pl.Buffered(1)),
        row_spec(D_NA), row_spec(D_NA), row_spec(D_WA), row_spec(D_WA_KV),
        pl.BlockSpec((MXU_DIM, MXU_DIM), const2),
    ]
    args = [xs, mod, norm_g, w_in, nqw, nkw, wqw, wkw, bd]
    if rope:
        in_specs += [pl.BlockSpec((tm, LANES), lambda b, t: (t, 0))] * 3
        args += list(rope_tabs)
    out_specs = [pl.BlockSpec((1, tm, n), lambda b, t: (b, t, 0)) for n in _PROJ_OUT_WIDTHS]
    out_shape = [jax.ShapeDtypeStruct((B, T, n), BF16) for n in _PROJ_OUT_WIDTHS]
    return pl.pallas_call(
        functools.partial(_proj_kernel, rope=rope),
        grid=(B, T // tm),
        in_specs=in_specs,
        out_specs=out_specs,
        out_shape=out_shape,
        scratch_shapes=[pltpu.VMEM((tm, D_MODEL), BF16)],
        compiler_params=_params(2),
        name="projection_rope" if rope else "projection_ctx",
    )(*args)


def _nt_dot(a, b):
    return lax.dot_general(a, b, (((1,), (1,)), ((), ())), preferred_element_type=F32)


def _low_half(rows):
    return lax.broadcasted_iota(jnp.int32, (rows, HEAD_PAIR), 1) < HEAD_DIM


def _keep_half(q_pair, half):
    lane = lax.broadcasted_iota(jnp.int32, (1, HEAD_PAIR), 1)
    keep = (lane < HEAD_DIM) if half == 0 else (lane >= HEAD_DIM)
    return q_pair * keep.astype(F32).astype(q_pair.dtype)


def _softmax_pv(scores, values, sinks=None):
    rows = scores.shape[0]
    n_chunks = len(sinks) if sinks else 1
    chunk = rows // n_chunks
    p_chunks, inv_chunks = [], []
    for c in range(n_chunks):
        s = scores[c * chunk:(c + 1) * chunk]
        m = jnp.max(s, axis=-1, keepdims=True)
        if sinks:
            m = jnp.maximum(m, sinks[c])
        p = jnp.exp2(s - m)
        den = jnp.sum(p, axis=-1, keepdims=True)
        if sinks:
            den = den + jnp.exp2(sinks[c] - m)
        p_chunks.append(p.astype(BF16))
        inv_chunks.append(1.0 / den)
    p = p_chunks[0] if n_chunks == 1 else jnp.concatenate(p_chunks, axis=0)
    inv = inv_chunks[0] if n_chunks == 1 else jnp.concatenate(inv_chunks, axis=0)
    return jnp.dot(p, values, preferred_element_type=F32), inv


def _pipelined(problems, scores_fn, finish_fn):
    outs = {}
    pending = scores_fn(*problems[0])
    for i, prob in enumerate(problems):
        nxt = scores_fn(*problems[i + 1]) if i + 1 < len(problems) else None
        outs[prob] = finish_fn(*prob, pending)
        pending = nxt
    return outs


NA_QROWS = 4
NA_KROWS = NA_QROWS + NA_MAX_ROWS


def _na_kernel(q_ref, k_ref, v_ref, kc_ref, vc_ref, bias_ref, o_ref, *, n_rows):
    t = pl.program_id(1)
    tq = NA_QROWS * GRID_W
    n_win = NA_KROWS * GRID_W
    kb = jnp.clip(t * NA_QROWS - NA_MAX_ROWS // 2, 0, n_rows - NA_KROWS)
    k0 = pl.multiple_of(kb * GRID_W, GRID_W)
    low = _low_half(tq)

    def scores(hp):
        lanes = slice(hp * HEAD_PAIR, (hp + 1) * HEAD_PAIR)
        q_pair = q_ref[0, :, lanes]
        q_stack = jnp.concatenate([_keep_half(q_pair, 0), _keep_half(q_pair, 1)], axis=0)
        keys = jnp.concatenate([k_ref[0, pl.ds(k0, n_win), lanes], kc_ref[0, :, lanes]], axis=0)
        s = _nt_dot(q_stack, keys)
        return jnp.concatenate([s[:, :n_win] + bias_ref[0, hp], s[:, n_win:]], axis=1)

    def finish(hp, sc):
        lanes = slice(hp * HEAD_PAIR, (hp + 1) * HEAD_PAIR)
        vals = jnp.concatenate([v_ref[0, pl.ds(k0, n_win), lanes], vc_ref[0, :, lanes]], axis=0)
        o, inv = _softmax_pv(sc, vals)
        o = o * inv
        o_ref[0, :, lanes] = jnp.where(low, o[:tq], o[tq:]).astype(BF16)
        return None

    _pipelined([(hp,) for hp in range(NA_HEADS // 2)], scores, finish)


def _na_attention(q, k, v, kc, vc, bias):
    B, S, _ = q.shape
    N = kc.shape[1]
    n_rows = S // GRID_W
    assert S % GRID_W == 0 and n_rows % NA_QROWS == 0 and n_rows >= NA_KROWS
    n_steps = n_rows // NA_QROWS
    tq = NA_QROWS * GRID_W
    full = lambda b, t: (b, 0, 0)
    variant = lambda b, t: (jnp.where(t == 0, 0, jnp.where(t == n_steps - 1, 2, 1)), 0, 0, 0)
    return pl.pallas_call(
        functools.partial(_na_kernel, n_rows=n_rows),
        grid=(B, n_steps),
        in_specs=[pl.BlockSpec((1, tq, D_NA), lambda b, t: (b, t, 0)),
                  pl.BlockSpec((1, S, D_NA), full),
                  pl.BlockSpec((1, S, D_NA), full),
                  pl.BlockSpec((1, N, D_NA), full),
                  pl.BlockSpec((1, N, D_NA), full),
                  pl.BlockSpec((1,) + bias.shape[1:], variant)],
        out_specs=pl.BlockSpec((1, tq, D_NA), lambda b, t: (b, t, 0)),
        out_shape=jax.ShapeDtypeStruct((B, S, D_NA), BF16),
        compiler_params=_params(2),
        name="na_attention",
    )(q, k, v, kc, vc, bias)


def _na_bias_table(rpb, n_rows):
    j = np.arange(GRID_W)[:, None]
    kc = np.arange(GRID_W)[None, :]
    cs = np.clip(j - NA_COLS // 2, 0, GRID_W - NA_COLS)
    col_ok = (kc >= cs) & (kc < cs + NA_COLS)
    c_off = np.clip(kc - j + (NA_COLS - 1), 0, 2 * NA_COLS - 2)
    by_off = jnp.where(col_ok[None, None], rpb.astype(F32)[:, :, c_off] * LOG2E, NEG_BIG).transpose(0, 2, 1, 3)
    tabs = []
    for i0 in (0, NA_QROWS, n_rows - NA_QROWS):
        kb = int(np.clip(i0 - NA_MAX_ROWS // 2, 0, n_rows - NA_KROWS))
        per_row = []
        for i in range(i0, i0 + NA_QROWS):
            rs = int(np.clip(i - NA_MAX_ROWS // 2, 0, n_rows - NA_MAX_ROWS))
            kr_lo, off_lo = rs - kb, rs - i + (NA_MAX_ROWS - 1)
            blk = by_off[:, :, off_lo:off_lo + NA_MAX_ROWS, :]
            per_row.append(jnp.pad(blk, ((0, 0), (0, 0), (kr_lo, NA_KROWS - NA_MAX_ROWS - kr_lo), (0, 0)),
                                   constant_values=NEG_BIG))
        tabs.append(jnp.stack(per_row, axis=1).reshape(
            NA_HEADS // 2, 2 * NA_QROWS * GRID_W, NA_KROWS * GRID_W))
    return jnp.stack(tabs)


def _wa_kernel(sink_ref, q_ref, k_ref, v_ref, kc_ref, vc_ref, o_ref, *, seq):
    n = pl.program_id(1)
    span = WA_BLOCK + 2 * WA_WINDOW
    ks = jnp.clip(n * WA_BLOCK - WA_WINDOW, 0, seq - span)
    k0 = pl.multiple_of(ks, WA_BLOCK)
    keys = jnp.concatenate([k_ref[0, pl.ds(k0, span), :], kc_ref[0]], axis=0)
    vals = jnp.concatenate([v_ref[0, pl.ds(k0, span), :], vc_ref[0]], axis=0)
    row = lax.broadcasted_iota(jnp.int32, (WA_BLOCK, span), 0)
    col = lax.broadcasted_iota(jnp.int32, (WA_BLOCK, span), 1)
    rel = col + (ks - n * WA_BLOCK) - row
    mask_add = jnp.where(jnp.abs(rel) <= WA_WINDOW, 0.0, NEG_BIG)
    mask_add = jnp.concatenate([mask_add] * WA_STACK, axis=0)
    low = _low_half(WA_BLOCK)
    problems = [(kh, gs) for kh in range(WA_KV_HEADS) for gs in range(0, WA_GROUP, WA_STACK)]

    def scores(kh, gs):
        q_stack = jnp.concatenate(
            [_keep_half(q_ref[0, :, g * HEAD_PAIR:(g + 1) * HEAD_PAIR], kh) for g in range(gs, gs + WA_STACK)],
            axis=0)
        s = _nt_dot(q_stack, keys)
        return jnp.concatenate([s[:, :span] + mask_add, s[:, span:]], axis=1)

    def finish(kh, gs, sc):
        sinks = [sink_ref[kh * WA_GROUP + g] * LOG2E for g in range(gs, gs + WA_STACK)]
        o, inv = _softmax_pv(sc, vals, sinks)
        return o * inv

    outs = _pipelined(problems, scores, finish)
    for g in range(WA_GROUP):
        gs, r0 = (g // WA_STACK) * WA_STACK, (g % WA_STACK) * WA_BLOCK
        lo = outs[(0, gs)][r0:r0 + WA_BLOCK]
        hi = outs[(1, gs)][r0:r0 + WA_BLOCK]
        o_ref[0, :, g * HEAD_PAIR:(g + 1) * HEAD_PAIR] = jnp.where(low, lo, hi).astype(BF16)


def _wa_attention(q, k, v, kc, vc, sink):
    B, S, _ = q.shape
    N = kc.shape[1]
    assert S % WA_BLOCK == 0 and S >= WA_BLOCK + 2 * WA_WINDOW
    full = lambda b, t: (b, 0, 0)
    return pl.pallas_call(
        functools.partial(_wa_kernel, seq=S),
        grid=(B, S // WA_BLOCK),
        in_specs=[pl.BlockSpec(memory_space=pltpu.SMEM),
                  pl.BlockSpec((1, WA_BLOCK, D_WA), lambda b, t: (b, t, 0)),
                  pl.BlockSpec((1, S, D_WA_KV), full),
                  pl.BlockSpec((1, S, D_WA_KV), full),
                  pl.BlockSpec((1, N, D_WA_KV), full),
                  pl.BlockSpec((1, N, D_WA_KV), full)],
        out_specs=pl.BlockSpec((1, WA_BLOCK, D_WA), lambda b, t: (b, t, 0)),
        out_shape=jax.ShapeDtypeStruct((B, S, D_WA), BF16),
        compiler_params=_params(2),
        name="wa_attention",
    )(sink, q, k, v, kc, vc)


def _ctx_attn_kernel(sink_ref, nq_ref, nk_ref, nv_ref, wq_ref, wk_ref, wv_ref, ona_ref, owa_ref):
    n_ctx = nq_ref.shape[1]
    low = _low_half(n_ctx)
    for hp in range(NA_HEADS // 2):
        lanes = slice(hp * HEAD_PAIR, (hp + 1) * HEAD_PAIR)
        q_pair = nq_ref[0, :, lanes]
        k_c = nk_ref[0, :, lanes]
        v_c = nv_ref[0, :, lanes]
        halves = []
        for half in range(2):
            qh = _keep_half(q_pair, half)
            o, inv = _softmax_pv(_nt_dot(qh, k_c), v_c)
            halves.append(o * inv)
        ona_ref[0, :, lanes] = jnp.where(low, halves[0], halves[1]).astype(BF16)
    k_c = wk_ref[0]
    v_c = wv_ref[0]
    for g in range(WA_GROUP):
        lanes = slice(g * HEAD_PAIR, (g + 1) * HEAD_PAIR)
        q_pair = wq_ref[0, :, lanes]
        halves = []
        for kh in range(WA_KV_HEADS):
            qh = _keep_half(q_pair, kh)
            o, inv = _softmax_pv(_nt_dot(qh, k_c), v_c, [sink_ref[kh * WA_GROUP + g] * LOG2E])
            halves.append(o * inv)
        owa_ref[0, :, lanes] = jnp.where(low, halves[0], halves[1]).astype(BF16)


def _ctx_attention(nq, nk, nv, wq, wk, wv, sink):
    B, N, _ = nq.shape
    blk = lambda n: pl.BlockSpec((1, N, n), lambda b: (b, 0, 0))
    return pl.pallas_call(
        _ctx_attn_kernel,
        grid=(B,),
        in_specs=[pl.BlockSpec(memory_space=pltpu.SMEM),
                  blk(D_NA), blk(D_NA), blk(D_NA), blk(D_WA), blk(D_WA_KV), blk(D_WA_KV)],
        out_specs=[blk(D_NA), blk(D_WA)],
        out_shape=[jax.ShapeDtypeStruct((B, N, D_NA), BF16), jax.ShapeDtypeStruct((B, N, D_WA), BF16)],
        compiler_params=_params(1),
        name="ctx_attention",
    )(sink, nq, nk, nv, wq, wk, wv)


CONV_HALO = 16
SUBLANES = 8
CONV_CHUNK = 64


def _conv_kernel(a_ref, ga_ref, w_ref, cb_ref, lg_ref, lb_ref, z_ref, pad_scr, shift_scr, y_scr,
                 *, tm, n_tiles):
    j = pl.program_id(1)
    t0 = pl.multiple_of(j * tm, tm)
    pad_scr[CONV_HALO:CONV_HALO + tm, :] = a_ref[0, pl.ds(t0, tm), :].astype(F32)
    zeros = jnp.zeros((CONV_HALO, D_CONV), F32)

    @pl.when(j > 0)
    def _():
        pad_scr[0:CONV_HALO, :] = a_ref[0, pl.ds(t0 - CONV_HALO, CONV_HALO), :].astype(F32)

    @pl.when(j == 0)
    def _():
        pad_scr[0:CONV_HALO, :] = zeros

    @pl.when(j < n_tiles - 1)
    def _():
        pad_scr[CONV_HALO + tm:, :] = a_ref[0, pl.ds(t0 + tm, CONV_HALO), :].astype(F32)

    @pl.when(j == n_tiles - 1)
    def _():
        pad_scr[CONV_HALO + tm:, :] = zeros

    n_shift = tm + CONV_HALO + SUBLANES
    for r in range(1, SUBLANES):
        shift_scr[r - 1] = pad_scr[r:r + n_shift, :]

    def tap(k, row0, lanes):
        r, base = (k + 1) % SUBLANES, (k + 1) // SUBLANES * SUBLANES
        if r == 0:
            return pad_scr[pl.ds(row0 + base, SUBLANES), lanes]
        return shift_scr[r - 1, pl.ds(row0 + base, SUBLANES), lanes]

    groups = CONV_CHUNK // SUBLANES
    for c in range(D_CONV // LANES):
        lanes = slice(c * LANES, (c + 1) * LANES)
        b_rows = jnp.broadcast_to(cb_ref[:, lanes], (SUBLANES, LANES))

        def chunk(g, carry, lanes=lanes, b_rows=b_rows):
            row0 = pl.multiple_of(g * CONV_CHUNK, CONV_CHUNK)
            sums = [[b_rows, None] for _ in range(groups)]
            for k in range(CONV_K):
                w_row = jnp.broadcast_to(w_ref[k:k + 1, lanes], (SUBLANES, LANES))
                for q in range(groups):
                    term = tap(k, row0 + q * SUBLANES, lanes) * w_row
                    prev = sums[q][k % 2]
                    sums[q][k % 2] = term if prev is None else prev + term
            for q in range(groups):
                y_scr[pl.ds(row0 + q * SUBLANES, SUBLANES), lanes] = sums[q][0] + sums[q][1]
            return carry

        lax.fori_loop(0, tm // CONV_CHUNK, chunk, 0)

    acc = y_scr[...]
    mu = jnp.mean(acc, axis=-1, keepdims=True)
    d = acc - mu
    var = jnp.mean(d * d, axis=-1, keepdims=True)
    y = d * lax.rsqrt(var + EPS) * lg_ref[...] + lb_ref[...]
    z_ref[0] = (_silu(y) * ga_ref[0].astype(F32)).astype(BF16)


def _conv_branch(a, ga, conv_w, conv_b, ln_g, ln_b, tm):
    B, T, _ = a.shape
    n_tiles = T // tm
    const2 = lambda b, t: (0, 0)
    row = pl.BlockSpec((1, D_CONV), const2)
    return pl.pallas_call(
        functools.partial(_conv_kernel, tm=tm, n_tiles=n_tiles),
        grid=(B, n_tiles),
        in_specs=[pl.BlockSpec((1, T, D_CONV), lambda b, t: (b, 0, 0)),
                  pl.BlockSpec((1, tm, D_CONV), lambda b, t: (b, t, 0)),
                  pl.BlockSpec((CONV_K, D_CONV), const2),
                  row, row, row],
        out_specs=pl.BlockSpec((1, tm, D_CONV), lambda b, t: (b, t, 0)),
        out_shape=jax.ShapeDtypeStruct((B, T, D_CONV), BF16),
        scratch_shapes=[pltpu.VMEM((tm + 2 * CONV_HALO, D_CONV), F32),
                        pltpu.VMEM((SUBLANES - 1, tm + CONV_HALO + SUBLANES, D_CONV), F32),
                        pltpu.VMEM((tm, D_CONV), F32)],
        compiler_params=_params(2),
        name="conv_branch",
    )(a, ga, conv_w, conv_b, ln_g, ln_b)


def _merge_kernel(z_ref, ona_ref, gn_ref, owa_ref, gw_ref, mg_ref, x_ref, mod_ref,
                  wa_ref, wb_ref, wc_ref, wo_ref, o_ref):
    def gated(o_r, g_r):
        return (o_r[0].astype(F32) * g_r[0].astype(F32)).astype(BF16)

    ya = jnp.dot(z_ref[0], wa_ref[...], preferred_element_type=F32)
    yb = jnp.dot(gated(ona_ref, gn_ref), wb_ref[...], preferred_element_type=F32)
    yc = jnp.dot(gated(owa_ref, gw_ref), wc_ref[...], preferred_element_type=F32)
    y = (mg_ref[0, :, 0:D_MODEL].astype(F32) * ya
         + mg_ref[0, :, D_MODEL:2 * D_MODEL].astype(F32) * yb
         + mg_ref[0, :, 2 * D_MODEL:].astype(F32) * yc)
    gate = mod_ref[0, :, 2 * D_MODEL:]
    o_ref[0] = x_ref[0] + gate * jnp.dot(y.astype(BF16), wo_ref[...], preferred_element_type=F32)


def _merge(z, ona, gn, owa, gw, mg, xs, mod, wa, wb, wc, wo, tm):
    B, T, _ = xs.shape
    tok = lambda n: pl.BlockSpec((1, tm, n), lambda b, t: (b, t, 0))
    const2 = lambda b, t: (0, 0)
    return pl.pallas_call(
        _merge_kernel,
        grid=(B, T // tm),
        in_specs=[tok(D_CONV), tok(D_NA), tok(D_NA), tok(D_WA), tok(D_WA), tok(3 * D_MODEL), tok(D_MODEL),
                  pl.BlockSpec((1, 1, 3 * D_MODEL), lambda b, t: (b, 0, 0)),
                  pl.BlockSpec((D_CONV, D_MODEL), const2),
                  pl.BlockSpec((D_NA, D_MODEL), const2),
                  pl.BlockSpec((D_WA, D_MODEL), const2),
                  pl.BlockSpec((D_MODEL, D_MODEL), const2)],
        out_specs=tok(D_MODEL),
        out_shape=jax.ShapeDtypeStruct((B, T, D_MODEL), F32),
        compiler_params=_params(2),
        name="merge",
    )(z, ona, gn, owa, gw, mg, xs, mod, wa, wb, wc, wo)


def _wa_head_order():
    order = []
    for p in range(WA_GROUP):
        order += [p, WA_GROUP + p]
    return np.asarray(order)


def _wa_column_perm():
    return (_wa_head_order()[:, None] * HEAD_DIM + np.arange(HEAD_DIM)[None, :]).reshape(-1)


def _rope_tables(seq):
    t = np.arange(seq)
    half = HEAD_DIM // 4
    inv = ROPE_BASE ** (-np.arange(half, dtype=np.float32) / half)
    lane = np.arange(LANES)
    d = lane % HEAD_DIM
    use_col = (d // (HEAD_DIM // 2)) == 1
    dd = d % (HEAD_DIM // 2)
    first = dd < half
    pos = jnp.where(use_col[None, :], (t % GRID_W)[:, None], (t // GRID_W)[:, None]).astype(F32)
    ang = pos * jnp.asarray(inv[dd % half], F32)[None, :]
    cos, sin = jnp.cos(ang), jnp.sin(ang)
    s1 = jnp.where(first[None, :], 0.0, sin)
    s2 = jnp.where(first[None, :], -sin, 0.0)
    return cos, s1, s2


def _block_diag_ones():
    i = np.arange(MXU_DIM)
    return jnp.asarray((i[:, None] // HEAD_DIM) == (i[None, :] // HEAD_DIM), BF16)


def _pick_tile(n, target):
    t = min(n, target)
    while n % t:
        t //= 2
    return t


def kernel(x, c, ctx, c_ctx, norm_g, w_ada, b_ada, w_in, conv_w, conv_b, cln_g, cln_b, w_proj_a,
           na_q_norm, na_k_norm, na_rpb, w_proj_b, wa_q_norm, wa_k_norm, wa_sink, w_proj_c, w_o):
    B, S, _ = x.shape
    N = ctx.shape[1]
    depth = w_in.shape[0]
    assert B <= 8
    c_all = jnp.zeros((16, D_MODEL), F32).at[:B].set(c).at[B].set(c_ctx)
    rope_tabs = _rope_tables(S)
    bd = _block_diag_ones()
    perm = _wa_column_perm()
    st = SPLIT_STARTS
    q_scale = HEAD_DIM ** -0.5 * LOG2E
    tm_x = _pick_tile(S, 512)
    tm_c = _pick_tile(N, 512)
    tile_row = lambda w, n: jnp.tile(w.astype(F32), n)[None, :]

    for l in range(depth):
        update_ctx = l < depth - 1
        mod = _modulation(c_all, w_ada[l], b_ada[l][None, :])
        mod_x = mod[:B, None, :]
        mod_c = jnp.broadcast_to(mod[B][None, None, :], (B, 1, 3 * D_MODEL))

        w_l = w_in[l]
        w_l = jnp.concatenate([w_l[:, :st[6]], w_l[:, st[6]:st[7]][:, perm], w_l[:, st[7]:st[9]],
                               w_l[:, st[9]:st[10]][:, perm], w_l[:, st[10]:]], axis=1).astype(BF16)
        nqw = tile_row(na_q_norm[l], NA_HEADS) * q_scale
        nkw = tile_row(na_k_norm[l], NA_HEADS)
        wqw = tile_row(wa_q_norm[l], WA_HEADS) * q_scale
        wkw = tile_row(wa_k_norm[l], WA_KV_HEADS)
        g_row = norm_g[l][None, :]

        (a, ga, nq, nk, nv, gn, wq, wk, wv, gw, mg) = _projection(
            x, mod_x, g_row, w_l, nqw, nkw, wqw, wkw, bd, rope_tabs, tm_x)
        (ac, gac, ncq, nkc, nvc, gnc, wcq, wkc, wvc, gwc, mgc) = _projection(
            ctx, mod_c, g_row, w_l, nqw, nkw, wqw, wkw, bd, None, tm_c)

        bias = _na_bias_table(na_rpb[l], S // GRID_W)
        sink = wa_sink[l].astype(F32)
        o_na = _na_attention(nq, nk, nv, nkc, nvc, bias)
        o_wa = _wa_attention(wq, wk, wv, wkc, wvc, sink)
        z = _conv_branch(a, ga, conv_w[l], conv_b[l][None, :], cln_g[l][None, :], cln_b[l][None, :],
                         _pick_tile(S, 256))
        wa_b = w_proj_a[l].astype(BF16)
        wb_b = w_proj_b[l].astype(BF16)
        wc_b = w_proj_c[l][perm, :].astype(BF16)
        wo_b = w_o[l].astype(BF16)
        x_new = _merge(z, o_na, gn, o_wa, gw, mg, x, mod_x, wa_b, wb_b, wc_b, wo_b, tm_x)

        if update_ctx:
            oc_na, oc_wa = _ctx_attention(ncq, nkc, nvc, wcq, wkc, wvc, sink)
            zc = _conv_branch(ac, gac, conv_w[l], conv_b[l][None, :], cln_g[l][None, :], cln_b[l][None, :],
                              _pick_tile(N, 256))
            ctx = _merge(zc, oc_na, gnc, oc_wa, gwc, mgc, ctx, mod_c, wa_b, wb_b, wc_b, wo_b, tm_c)
        x = x_new
    return x
```

```python
import functools

import numpy as np
import jax
import jax.numpy as jnp
from jax import lax
from jax.experimental import pallas as pl
from jax.experimental.pallas import tpu as pltpu

D_MODEL = 1024
GRID_W = 64
HEAD_DIM = 64
D_CONV = 512
CONV_K = 31
NA_HEADS = 8
NA_MAX_ROWS = 8
NA_COLS = 16
WA_HEADS = 8
WA_KV_HEADS = 2
WA_GROUP = WA_HEADS // WA_KV_HEADS
WA_WINDOW = 128
WA_BLOCK = 128
ROPE_BASE = 10000.0
EPS = 1e-6
D_NA = NA_HEADS * HEAD_DIM
D_WA = WA_HEADS * HEAD_DIM
D_WA_KV = WA_KV_HEADS * HEAD_DIM
SPLIT_SIZES = (2 * D_CONV, D_CONV, D_NA, D_NA, D_NA, D_NA, D_WA, D_WA_KV, D_WA_KV, D_WA, 3 * D_MODEL)
D_IN = sum(SPLIT_SIZES)
SPLIT_STARTS = tuple(int(v) for v in np.cumsum((0,) + SPLIT_SIZES[:-1]))

LANES = 128
MXU_DIM = 256
HEAD_PAIR = 2 * HEAD_DIM
ROPE_HALF = HEAD_DIM // 4
NEG_BIG = -1e30
LOG2E = 1.4426950408889634
VMEM_LIMIT = 56 * 1024 * 1024

F32 = jnp.float32
BF16 = jnp.bfloat16


def _sigmoid(v):
    return 1.0 / (1.0 + jnp.exp(-v))


def _silu(v):
    return v * _sigmoid(v)


def _params(n_grid):
    return pltpu.CompilerParams(dimension_semantics=("arbitrary",) * n_grid,
                                vmem_limit_bytes=VMEM_LIMIT)


def _mod_kernel(c_ref, w_ref, b_ref, o_ref):
    s = _silu(c_ref[...])
    o_ref[...] = jnp.dot(s.astype(BF16), w_ref[...].astype(BF16),
                         preferred_element_type=F32) + b_ref[...]


def _modulation(c_all, w_ada, b_ada):
    rows = c_all.shape[0]
    n_out = w_ada.shape[1]
    blk = D_MODEL
    return pl.pallas_call(
        _mod_kernel,
        grid=(n_out // blk,),
        in_specs=[pl.BlockSpec((rows, D_MODEL), lambda j: (0, 0)),
                  pl.BlockSpec((D_MODEL, blk), lambda j: (0, j)),
                  pl.BlockSpec((1, blk), lambda j: (0, j))],
        out_specs=pl.BlockSpec((rows, blk), lambda j: (0, j)),
        out_shape=jax.ShapeDtypeStruct((rows, n_out), F32),
        compiler_params=_params(1),
        name="modulation",
    )(c_all, w_ada, b_ada)


def _proj_kernel(*refs, rope, kv_only):
    if rope:
        (x_ref, mod_ref, g_ref, w_ref, nqw_ref, nkw_ref, wqw_ref, wkw_ref, bd_ref,
         cos_ref, s1_ref, s2_ref) = refs[:12]
        outs = refs[12:]
    else:
        (x_ref, mod_ref, g_ref, w_ref, nqw_ref, nkw_ref, wqw_ref, wkw_ref, bd_ref) = refs[:9]
        cos_ref = s1_ref = s2_ref = None
        outs = refs[9:]
    if kv_only:
        nk_ref, nv_ref, wk_ref, wv_ref, h_scr = outs
    else:
        (a_ref, ga_ref, nq_ref, nk_ref, nv_ref, gn_ref, wq_ref, wk_ref, wv_ref, gw_ref, mg_ref,
         h_scr) = outs

    x = x_ref[0]
    ms = jnp.mean(x * x, axis=-1, keepdims=True)
    shift = mod_ref[0, :, 0:D_MODEL]
    scale = mod_ref[0, :, D_MODEL:2 * D_MODEL]
    h = (x * lax.rsqrt(ms + EPS) * g_ref[...]) * (1.0 + scale) + shift
    h_scr[...] = h.astype(BF16)

    st = SPLIT_STARTS

    def seg(c0, n):
        return jnp.dot(h_scr[...], w_ref[:, c0:c0 + n], preferred_element_type=F32)

    def head_norm(t, w_row_ref):
        n = t.shape[1]
        sq = (t * t).astype(BF16)
        if n >= MXU_DIM:
            parts = [jnp.dot(sq[:, c:c + MXU_DIM], bd_ref[...], preferred_element_type=F32)
                     for c in range(0, n, MXU_DIM)]
            ss = jnp.concatenate(parts, axis=1) if len(parts) > 1 else parts[0]
        else:
            ss = jnp.dot(sq, bd_ref[0:n, 0:n], preferred_element_type=F32)
        return t * lax.rsqrt(ss * (1.0 / HEAD_DIM) + EPS) * w_row_ref[...]

    def rotary(tc):
        if not rope:
            return tc
        lo = pltpu.roll(tc, ROPE_HALF, axis=1)
        hi = pltpu.roll(tc, LANES - ROPE_HALF, axis=1)
        return tc * cos_ref[...] + lo * s1_ref[...] + hi * s2_ref[...]

    def store_kv_dup(t, o_ref):
        sw = pltpu.roll(t, HEAD_DIM, axis=1)
        low = _low_half(t.shape[0])
        o_ref[0, :, 0:LANES] = jnp.where(low, t, sw).astype(BF16)
        o_ref[0, :, LANES:2 * LANES] = jnp.where(low, sw, t).astype(BF16)

    nk_ref[0] = head_norm(seg(st[3], D_NA), nkw_ref).astype(BF16)
    nv_ref[0] = seg(st[4], D_NA).astype(BF16)
    store_kv_dup(rotary(head_norm(seg(st[7], D_WA_KV), wkw_ref)), wk_ref)
    store_kv_dup(seg(st[8], D_WA_KV), wv_ref)
    if kv_only:
        return
    glu = seg(st[0], 2 * D_CONV)
    a_ref[0] = (glu[:, :D_CONV] * _sigmoid(glu[:, D_CONV:])).astype(BF16)
    ga_ref[0] = _silu(seg(st[1], D_CONV)).astype(BF16)
    nq_ref[0] = head_norm(seg(st[2], D_NA), nqw_ref).astype(BF16)
    gn_ref[0] = _silu(seg(st[5], D_NA)).astype(BF16)
    wq = head_norm(seg(st[6], D_WA), wqw_ref)
    for c in range(0, D_WA, LANES):
        wq_ref[0, :, c:c + LANES] = rotary(wq[:, c:c + LANES]).astype(BF16)
    gw_ref[0] = _silu(seg(st[9], D_WA)).astype(BF16)
    for c in range(0, 3 * D_MODEL, D_CONV):
        mg_ref[0, :, c:c + D_CONV] = _sigmoid(seg(st[10] + c, D_CONV)).astype(BF16)


_PROJ_OUT_WIDTHS = (D_CONV, D_CONV, D_NA, D_NA, D_NA, D_NA, D_WA, 2 * D_WA_KV, 2 * D_WA_KV, D_WA, 3 * D_MODEL)
_PROJ_KV_WIDTHS = (D_NA, D_NA, 2 * D_WA_KV, 2 * D_WA_KV)


def _projection(xs, mod, norm_g, w_in, nqw, nkw, wqw, wkw, bd, rope_tabs, tm, kv_only=False):
    B, T, _ = xs.shape
    rope = rope_tabs is not None
    widths = _PROJ_KV_WIDTHS if kv_only else _PROJ_OUT_WIDTHS
    const2 = lambda b, t: (0, 0)
    row_spec = lambda n: pl.BlockSpec((1, n), const2)
    in_specs = [
        pl.BlockSpec((1, tm, D_MODEL), lambda b, t: (b, t, 0)),
        pl.BlockSpec((1, 1, 3 * D_MODEL), lambda b, t: (b, 0, 0)),
        row_spec(D_MODEL),
        pl.BlockSpec((D_MODEL, D_IN), const2, pipeline_mode=pl.Buffered(1)),
        row_spec(D_NA), row_spec(D_NA), row_spec(D_WA), row_spec(D_WA_KV),
        pl.BlockSpec((MXU_DIM, MXU_DIM), const2),
    ]
    args = [xs, mod, norm_g, w_in, nqw, nkw, wqw, wkw, bd]
    if rope:
        in_specs += [pl.BlockSpec((tm, LANES), lambda b, t: (t, 0))] * 3
        args += list(rope_tabs)
    out_specs = [pl.BlockSpec((1, tm, n), lambda b, t: (b, t, 0)) for n in widths]
    out_shape = [jax.ShapeDtypeStruct((B, T, n), BF16) for n in widths]
    return pl.pallas_call(
        functools.partial(_proj_kernel, rope=rope, kv_only=kv_only),
        grid=(B, T // tm),
        in_specs=in_specs,
        out_specs=out_specs,
        out_shape=out_shape,
        scratch_shapes=[pltpu.VMEM((tm, D_MODEL), BF16)],
        compiler_params=_params(2),
        name="projection_rope" if rope else ("projection_ctx_kv" if kv_only else "projection_ctx"),
    )(*args)


def _nt_dot(a, b):
    return lax.dot_general(a, b, (((1,), (1,)), ((), ())), preferred_element_type=F32)


def _low_half(rows):
    return lax.broadcasted_iota(jnp.int32, (rows, HEAD_PAIR), 1) < HEAD_DIM


def _keep_half(q_pair, half):
    lane = lax.broadcasted_iota(jnp.int32, (1, HEAD_PAIR), 1)
    keep = (lane < HEAD_DIM) if half == 0 else (lane >= HEAD_DIM)
    return q_pair * keep.astype(F32).astype(q_pair.dtype)


def _softmax_pv(scores, values, sinks=None):
    rows = scores.shape[0]
    n_chunks = len(sinks) if sinks else 1
    chunk = rows // n_chunks
    p_chunks, inv_chunks = [], []
    for c in range(n_chunks):
        s = scores[c * chunk:(c + 1) * chunk]
        m = jnp.max(s, axis=-1, keepdims=True)
        if sinks:
            m = jnp.maximum(m, sinks[c])
        p = jnp.exp2(s - m)
        den = jnp.sum(p, axis=-1, keepdims=True)
        if sinks:
            den = den + jnp.exp2(sinks[c] - m)
        p_chunks.append(p.astype(BF16))
        inv_chunks.append(1.0 / den)
    p = p_chunks[0] if n_chunks == 1 else jnp.concatenate(p_chunks, axis=0)
    inv = inv_chunks[0] if n_chunks == 1 else jnp.concatenate(inv_chunks, axis=0)
    return jnp.dot(p, values, preferred_element_type=F32), inv


PIPELINE_AHEAD = 1


def _pipelined(problems, scores_fn, finish_fn):
    pending = [scores_fn(*p) for p in problems[:PIPELINE_AHEAD]]
    for i, prob in enumerate(problems):
        if i + PIPELINE_AHEAD < len(problems):
            pending.append(scores_fn(*problems[i + PIPELINE_AHEAD]))
        finish_fn(*prob, pending.pop(0))


NA_QROWS = 4
NA_KROWS = NA_QROWS + NA_MAX_ROWS


def _na_kernel(q_ref, k_ref, v_ref, kc_ref, vc_ref, bias_ref, o_ref, *, n_rows):
    t = pl.program_id(1)
    tq = NA_QROWS * GRID_W
    n_win = NA_KROWS * GRID_W
    kb = jnp.clip(t * NA_QROWS - NA_MAX_ROWS // 2, 0, n_rows - NA_KROWS)
    k0 = pl.multiple_of(kb * GRID_W, GRID_W)
    low = _low_half(tq)

    held = {}

    def scores(hp, half):
        lanes = slice(hp * HEAD_PAIR, (hp + 1) * HEAD_PAIR)
        q_half = _keep_half(q_ref[0, :, lanes], half)
        keys = jnp.concatenate([k_ref[0, pl.ds(k0, n_win), lanes], kc_ref[0, :, lanes]], axis=0)
        s = _nt_dot(q_half, keys)
        return jnp.concatenate([s[:, :n_win] + bias_ref[0, hp, half * tq:(half + 1) * tq], s[:, n_win:]], axis=1)

    def finish(hp, half, sc):
        lanes = slice(hp * HEAD_PAIR, (hp + 1) * HEAD_PAIR)
        vals = jnp.concatenate([v_ref[0, pl.ds(k0, n_win), lanes], vc_ref[0, :, lanes]], axis=0)
        o, inv = _softmax_pv(sc, vals)
        o = o * inv
        if half == 0:
            held[hp] = o
        else:
            o_ref[0, :, lanes] = jnp.where(low, held.pop(hp), o).astype(BF16)

    _pipelined([(hp, half) for hp in range(NA_HEADS // 2) for half in range(2)], scores, finish)


def _na_attention(q, k, v, kc, vc, bias):
    B, S, _ = q.shape
    N = kc.shape[1]
    n_rows = S // GRID_W
    assert S % GRID_W == 0 and n_rows % NA_QROWS == 0 and n_rows >= NA_KROWS
    n_steps = n_rows // NA_QROWS
    tq = NA_QROWS * GRID_W
    full = lambda b, t: (b, 0, 0)
    variant = lambda b, t: (jnp.where(t == 0, 0, jnp.where(t == n_steps - 1, 2, 1)), 0, 0, 0)
    return pl.pallas_call(
        functools.partial(_na_kernel, n_rows=n_rows),
        grid=(B, n_steps),
        in_specs=[pl.BlockSpec((1, tq, D_NA), lambda b, t: (b, t, 0)),
                  pl.BlockSpec((1, S, D_NA), full),
                  pl.BlockSpec((1, S, D_NA), full),
                  pl.BlockSpec((1, N, D_NA), full),
                  pl.BlockSpec((1, N, D_NA), full),
                  pl.BlockSpec((1,) + bias.shape[1:], variant)],
        out_specs=pl.BlockSpec((1, tq, D_NA), lambda b, t: (b, t, 0)),
        out_shape=jax.ShapeDtypeStruct((B, S, D_NA), BF16),
        compiler_params=_params(2),
        name="na_attention",
    )(q, k, v, kc, vc, bias)


def _na_bias_table(rpb, n_rows):
    j = np.arange(GRID_W)[:, None]
    kc = np.arange(GRID_W)[None, :]
    cs = np.clip(j - NA_COLS // 2, 0, GRID_W - NA_COLS)
    col_ok = (kc >= cs) & (kc < cs + NA_COLS)
    c_off = np.clip(kc - j + (NA_COLS - 1), 0, 2 * NA_COLS - 2)
    by_off = jnp.where(col_ok[None, None], rpb.astype(F32)[:, :, c_off] * LOG2E, NEG_BIG)
    by_off = by_off.transpose(0, 2, 1, 3).reshape(NA_HEADS, GRID_W, (2 * NA_MAX_ROWS - 1) * GRID_W)
    tabs = []
    for i0 in (0, NA_QROWS, n_rows - NA_QROWS):
        kb = int(np.clip(i0 - NA_MAX_ROWS // 2, 0, n_rows - NA_KROWS))
        per_row = []
        for i in range(i0, i0 + NA_QROWS):
            rs = int(np.clip(i - NA_MAX_ROWS // 2, 0, n_rows - NA_MAX_ROWS))
            kr_lo, off_lo = rs - kb, rs - i + (NA_MAX_ROWS - 1)
            blk = by_off[:, :, off_lo * GRID_W:(off_lo + NA_MAX_ROWS) * GRID_W]
            per_row.append(jnp.pad(
                blk, ((0, 0), (0, 0), (kr_lo * GRID_W, (NA_KROWS - NA_MAX_ROWS - kr_lo) * GRID_W)),
                constant_values=NEG_BIG))
        tabs.append(jnp.stack(per_row, axis=1).reshape(
            NA_HEADS // 2, 2 * NA_QROWS * GRID_W, NA_KROWS * GRID_W))
    return jnp.stack(tabs)


WA_QBLOCKS = 2


def _wa_kernel(sink_ref, q_ref, k_ref, v_ref, kc_ref, vc_ref, o_ref, *, seq):
    step = pl.program_id(1)
    span = WA_BLOCK + 2 * WA_WINDOW
    low = _low_half(WA_BLOCK)
    row = lax.broadcasted_iota(jnp.int32, (WA_BLOCK, span), 0)
    col = lax.broadcasted_iota(jnp.int32, (WA_BLOCK, span), 1)
    key0, mask_add = [], []
    for sb in range(WA_QBLOCKS):
        n = step * WA_QBLOCKS + sb
        ks = jnp.clip(n * WA_BLOCK - WA_WINDOW, 0, seq - span)
        rel = col + (ks - n * WA_BLOCK) - row
        m = jnp.where(jnp.abs(rel) <= WA_WINDOW, 0.0, NEG_BIG)
        key0.append(pl.multiple_of(ks, WA_BLOCK))
        mask_add.append(jnp.concatenate([m, m], axis=0))

    def kv_lanes(pair):
        kh = pair * 2 // WA_GROUP
        return slice(kh * HEAD_PAIR, (kh + 1) * HEAD_PAIR)

    def scores(sb, pair):
        q_pair = q_ref[0, sb * WA_BLOCK:(sb + 1) * WA_BLOCK, pair * HEAD_PAIR:(pair + 1) * HEAD_PAIR]
        q_stack = jnp.concatenate([_keep_half(q_pair, 0), _keep_half(q_pair, 1)], axis=0)
        keys = jnp.concatenate(
            [k_ref[0, pl.ds(key0[sb], span), kv_lanes(pair)], kc_ref[0, :, kv_lanes(pair)]], axis=0)
        s = _nt_dot(q_stack, keys)
        return jnp.concatenate([s[:, :span] + mask_add[sb], s[:, span:]], axis=1)

    def finish(sb, pair, sc):
        vals = jnp.concatenate(
            [v_ref[0, pl.ds(key0[sb], span), kv_lanes(pair)], vc_ref[0, :, kv_lanes(pair)]], axis=0)
        sinks = [sink_ref[2 * pair + half] * LOG2E for half in range(2)]
        o, inv = _softmax_pv(sc, vals, sinks)
        o = o * inv
        o_ref[0, sb * WA_BLOCK:(sb + 1) * WA_BLOCK, pair * HEAD_PAIR:(pair + 1) * HEAD_PAIR] = jnp.where(
            low, o[:WA_BLOCK], o[WA_BLOCK:]).astype(BF16)

    _pipelined([(sb, pair) for sb in range(WA_QBLOCKS) for pair in range(WA_HEADS // 2)], scores, finish)


def _wa_attention(q, k, v, kc, vc, sink):
    B, S, _ = q.shape
    N = kc.shape[1]
    tq = WA_QBLOCKS * WA_BLOCK
    assert S % tq == 0 and S >= WA_BLOCK + 2 * WA_WINDOW
    full = lambda b, t: (b, 0, 0)
    return pl.pallas_call(
        functools.partial(_wa_kernel, seq=S),
        grid=(B, S // tq),
        in_specs=[pl.BlockSpec(memory_space=pltpu.SMEM),
                  pl.BlockSpec((1, tq, D_WA), lambda b, t: (b, t, 0)),
                  pl.BlockSpec((1, S, 2 * D_WA_KV), full),
                  pl.BlockSpec((1, S, 2 * D_WA_KV), full),
                  pl.BlockSpec((1, N, 2 * D_WA_KV), full),
                  pl.BlockSpec((1, N, 2 * D_WA_KV), full)],
        out_specs=pl.BlockSpec((1, tq, D_WA), lambda b, t: (b, t, 0)),
        out_shape=jax.ShapeDtypeStruct((B, S, D_WA), BF16),
        compiler_params=_params(2),
        name="wa_attention",
    )(sink, q, k, v, kc, vc)


def _ctx_attn_kernel(sink_ref, nq_ref, nk_ref, nv_ref, wq_ref, wk_ref, wv_ref, ona_ref, owa_ref):
    n_ctx = nq_ref.shape[1]
    low = _low_half(n_ctx)
    for hp in range(NA_HEADS // 2):
        lanes = slice(hp * HEAD_PAIR, (hp + 1) * HEAD_PAIR)
        q_pair = nq_ref[0, :, lanes]
        k_c = nk_ref[0, :, lanes]
        v_c = nv_ref[0, :, lanes]
        halves = []
        for half in range(2):
            qh = _keep_half(q_pair, half)
            o, inv = _softmax_pv(_nt_dot(qh, k_c), v_c)
            halves.append(o * inv)
        ona_ref[0, :, lanes] = jnp.where(low, halves[0], halves[1]).astype(BF16)
    for pair in range(WA_HEADS // 2):
        lanes = slice(pair * HEAD_PAIR, (pair + 1) * HEAD_PAIR)
        kh = pair * 2 // WA_GROUP
        kv_lanes = slice(kh * HEAD_PAIR, (kh + 1) * HEAD_PAIR)
        q_pair = wq_ref[0, :, lanes]
        k_c = wk_ref[0, :, kv_lanes]
        v_c = wv_ref[0, :, kv_lanes]
        halves = []
        for half in range(2):
            qh = _keep_half(q_pair, half)
            o, inv = _softmax_pv(_nt_dot(qh, k_c), v_c, [sink_ref[2 * pair + half] * LOG2E])
            halves.append(o * inv)
        owa_ref[0, :, lanes] = jnp.where(low, halves[0], halves[1]).astype(BF16)


def _ctx_attention(nq, nk, nv, wq, wk, wv, sink):
    B, N, _ = nq.shape
    blk = lambda n: pl.BlockSpec((1, N, n), lambda b: (b, 0, 0))
    return pl.pallas_call(
        _ctx_attn_kernel,
        grid=(B,),
        in_specs=[pl.BlockSpec(memory_space=pltpu.SMEM),
                  blk(D_NA), blk(D_NA), blk(D_NA), blk(D_WA), blk(2 * D_WA_KV), blk(2 * D_WA_KV)],
        out_specs=[blk(D_NA), blk(D_WA)],
        out_shape=[jax.ShapeDtypeStruct((B, N, D_NA), BF16), jax.ShapeDtypeStruct((B, N, D_WA), BF16)],
        compiler_params=_params(1),
        name="ctx_attention",
    )(sink, nq, nk, nv, wq, wk, wv)


CONV_HALO = 16
SUBLANES = 8
CONV_CHUNK = 64


def _conv_kernel(a_ref, ga_ref, w_ref, cb_ref, lg_ref, lb_ref, z_ref, pad_scr, shift_scr, y_scr,
                 *, tm, n_tiles):
    j = pl.program_id(1)
    t0 = pl.multiple_of(j * tm, tm)
    pad_scr[CONV_HALO:CONV_HALO + tm, :] = a_ref[0, pl.ds(t0, tm), :].astype(F32)
    zeros = jnp.zeros((CONV_HALO, D_CONV), F32)

    @pl.when(j > 0)
    def _():
        pad_scr[0:CONV_HALO, :] = a_ref[0, pl.ds(t0 - CONV_HALO, CONV_HALO), :].astype(F32)

    @pl.when(j == 0)
    def _():
        pad_scr[0:CONV_HALO, :] = zeros

    @pl.when(j < n_tiles - 1)
    def _():
        pad_scr[CONV_HALO + tm:, :] = a_ref[0, pl.ds(t0 + tm, CONV_HALO), :].astype(F32)

    @pl.when(j == n_tiles - 1)
    def _():
        pad_scr[CONV_HALO + tm:, :] = zeros

    n_shift = tm + CONV_HALO + SUBLANES
    for r in range(1, SUBLANES):
        shift_scr[r - 1] = pad_scr[r:r + n_shift, :]

    def tap(k, row0, lanes):
        r, base = (k + 1) % SUBLANES, (k + 1) // SUBLANES * SUBLANES
        if r == 0:
            return pad_scr[pl.ds(row0 + base, SUBLANES), lanes]
        return shift_scr[r - 1, pl.ds(row0 + base, SUBLANES), lanes]

    groups = CONV_CHUNK // SUBLANES
    for c in range(D_CONV // LANES):
        lanes = slice(c * LANES, (c + 1) * LANES)
        b_rows = jnp.broadcast_to(cb_ref[:, lanes], (SUBLANES, LANES))

        def chunk(g, carry, lanes=lanes, b_rows=b_rows):
            row0 = pl.multiple_of(g * CONV_CHUNK, CONV_CHUNK)
            sums = [[b_rows, None] for _ in range(groups)]
            for k in range(CONV_K):
                w_row = jnp.broadcast_to(w_ref[k:k + 1, lanes], (SUBLANES, LANES))
                for q in range(groups):
                    term = tap(k, row0 + q * SUBLANES, lanes) * w_row
                    prev = sums[q][k % 2]
                    sums[q][k % 2] = term if prev is None else prev + term
            for q in range(groups):
                y_scr[pl.ds(row0 + q * SUBLANES, SUBLANES), lanes] = sums[q][0] + sums[q][1]
            return carry

        lax.fori_loop(0, tm // CONV_CHUNK, chunk, 0)

    acc = y_scr[...]
    mu = jnp.mean(acc, axis=-1, keepdims=True)
    d = acc - mu
    var = jnp.mean(d * d, axis=-1, keepdims=True)
    y = d * lax.rsqrt(var + EPS) * lg_ref[...] + lb_ref[...]
    z_ref[0] = (_silu(y) * ga_ref[0].astype(F32)).astype(BF16)


def _conv_branch(a, ga, conv_w, conv_b, ln_g, ln_b, tm):
    B, T, _ = a.shape
    n_tiles = T // tm
    const2 = lambda b, t: (0, 0)
    row = pl.BlockSpec((1, D_CONV), const2)
    return pl.pallas_call(
        functools.partial(_conv_kernel, tm=tm, n_tiles=n_tiles),
        grid=(B, n_tiles),
        in_specs=[pl.BlockSpec((1, T, D_CONV), lambda b, t: (b, 0, 0)),
                  pl.BlockSpec((1, tm, D_CONV), lambda b, t: (b, t, 0)),
                  pl.BlockSpec((CONV_K, D_CONV), const2),
                  row, row, row],
        out_specs=pl.BlockSpec((1, tm, D_CONV), lambda b, t: (b, t, 0)),
        out_shape=jax.ShapeDtypeStruct((B, T, D_CONV), BF16),
        scratch_shapes=[pltpu.VMEM((tm + 2 * CONV_HALO, D_CONV), F32),
                        pltpu.VMEM((SUBLANES - 1, tm + CONV_HALO + SUBLANES, D_CONV), F32),
                        pltpu.VMEM((tm, D_CONV), F32)],
        compiler_params=_params(2),
        name="conv_branch",
    )(a, ga, conv_w, conv_b, ln_g, ln_b)


def _merge_kernel(z_ref, ona_ref, gn_ref, owa_ref, gw_ref, mg_ref, x_ref, mod_ref,
                  wa_ref, wb_ref, wc_ref, wo_ref, o_ref):
    def gated(o_r, g_r):
        return (o_r[0].astype(F32) * g_r[0].astype(F32)).astype(BF16)

    ya = jnp.dot(z_ref[0], wa_ref[...], preferred_element_type=F32)
    yb = jnp.dot(gated(ona_ref, gn_ref), wb_ref[...], preferred_element_type=F32)
    yc = jnp.dot(gated(owa_ref, gw_ref), wc_ref[...], preferred_element_type=F32)
    y = (mg_ref[0, :, 0:D_MODEL].astype(F32) * ya
         + mg_ref[0, :, D_MODEL:2 * D_MODEL].astype(F32) * yb
         + mg_ref[0, :, 2 * D_MODEL:].astype(F32) * yc)
    gate = mod_ref[0, :, 2 * D_MODEL:]
    o_ref[0] = x_ref[0] + gate * jnp.dot(y.astype(BF16), wo_ref[...], preferred_element_type=F32)


def _merge(z, ona, gn, owa, gw, mg, xs, mod, wa, wb, wc, wo, tm):
    B, T, _ = xs.shape
    tok = lambda n: pl.BlockSpec((1, tm, n), lambda b, t: (b, t, 0))
    const2 = lambda b, t: (0, 0)
    return pl.pallas_call(
        _merge_kernel,
        grid=(B, T // tm),
        in_specs=[tok(D_CONV), tok(D_NA), tok(D_NA), tok(D_WA), tok(D_WA), tok(3 * D_MODEL), tok(D_MODEL),
                  pl.BlockSpec((1, 1, 3 * D_MODEL), lambda b, t: (b, 0, 0)),
                  pl.BlockSpec((D_CONV, D_MODEL), const2),
                  pl.BlockSpec((D_NA, D_MODEL), const2),
                  pl.BlockSpec((D_WA, D_MODEL), const2),
                  pl.BlockSpec((D_MODEL, D_MODEL), const2)],
        out_specs=tok(D_MODEL),
        out_shape=jax.ShapeDtypeStruct((B, T, D_MODEL), F32),
        compiler_params=_params(2),
        name="merge",
    )(z, ona, gn, owa, gw, mg, xs, mod, wa, wb, wc, wo)


def _rope_tables(seq):
    t = np.arange(seq)
    half = HEAD_DIM // 4
    inv = ROPE_BASE ** (-np.arange(half, dtype=np.float32) / half)
    lane = np.arange(LANES)
    d = lane % HEAD_DIM
    use_col = (d // (HEAD_DIM // 2)) == 1
    dd = d % (HEAD_DIM // 2)
    first = dd < half
    pos = jnp.where(use_col[None, :], (t % GRID_W)[:, None], (t // GRID_W)[:, None]).astype(F32)
    ang = pos * jnp.asarray(inv[dd % half], F32)[None, :]
    cos, sin = jnp.cos(ang), jnp.sin(ang)
    s1 = jnp.where(first[None, :], 0.0, sin)
    s2 = jnp.where(first[None, :], -sin, 0.0)
    return cos, s1, s2


def _block_diag_ones():
    i = np.arange(MXU_DIM)
    return jnp.asarray((i[:, None] // HEAD_DIM) == (i[None, :] // HEAD_DIM), BF16)


def _pick_tile(n, target):
    t = min(n, target)
    while n % t:
        t //= 2
    return t


def kernel(x, c, ctx, c_ctx, norm_g, w_ada, b_ada, w_in, conv_w, conv_b, cln_g, cln_b, w_proj_a,
           na_q_norm, na_k_norm, na_rpb, w_proj_b, wa_q_norm, wa_k_norm, wa_sink, w_proj_c, w_o):
    B, S, _ = x.shape
    N = ctx.shape[1]
    depth = w_in.shape[0]
    assert B <= 8
    c_all = jnp.zeros((16, D_MODEL), F32).at[:B].set(c).at[B].set(c_ctx)
    rope_tabs = _rope_tables(S)
    bd = _block_diag_ones()
    q_scale = HEAD_DIM ** -0.5 * LOG2E
    tm_x = _pick_tile(S, 512)
    tm_c = _pick_tile(N, 512)
    tile_row = lambda w, n: jnp.tile(w.astype(F32), n)[None, :]

    for l in range(depth):
        update_ctx = l < depth - 1
        mod = _modulation(c_all, w_ada[l], b_ada[l][None, :])
        mod_x = mod[:B, None, :]
        mod_c = jnp.broadcast_to(mod[B][None, None, :], (B, 1, 3 * D_MODEL))

        w_l = w_in[l].astype(BF16)
        nqw = tile_row(na_q_norm[l], NA_HEADS) * q_scale
        nkw = tile_row(na_k_norm[l], NA_HEADS)
        wqw = tile_row(wa_q_norm[l], WA_HEADS) * q_scale
        wkw = tile_row(wa_k_norm[l], WA_KV_HEADS)
        g_row = norm_g[l][None, :]

        (a, ga, nq, nk, nv, gn, wq, wk, wv, gw, mg) = _projection(
            x, mod_x, g_row, w_l, nqw, nkw, wqw, wkw, bd, rope_tabs, tm_x)
        if update_ctx:
            (ac, gac, ncq, nkc, nvc, gnc, wcq, wkc, wvc, gwc, mgc) = _projection(
                ctx, mod_c, g_row, w_l, nqw, nkw, wqw, wkw, bd, None, tm_c)
        else:
            nkc, nvc, wkc, wvc = _projection(
                ctx, mod_c, g_row, w_l, nqw, nkw, wqw, wkw, bd, None, tm_c, kv_only=True)

        bias = _na_bias_table(na_rpb[l], S // GRID_W)
        sink = wa_sink[l].astype(F32)
        o_na = _na_attention(nq, nk, nv, nkc, nvc, bias)
        o_wa = _wa_attention(wq, wk, wv, wkc, wvc, sink)
        z = _conv_branch(a, ga, conv_w[l], conv_b[l][None, :], cln_g[l][None, :], cln_b[l][None, :],
                         _pick_tile(S, 256))
        wa_b = w_proj_a[l].astype(BF16)
        wb_b = w_proj_b[l].astype(BF16)
        wc_b = w_proj_c[l].astype(BF16)
        wo_b = w_o[l].astype(BF16)
        x_new = _merge(z, o_na, gn, o_wa, gw, mg, x, mod_x, wa_b, wb_b, wc_b, wo_b, tm_x)

        if update_ctx:
            oc_na, oc_wa = _ctx_attention(ncq, nkc, nvc, wcq, wkc, wvc, sink)
            zc = _conv_branch(ac, gac, conv_w[l], conv_b[l][None, :], cln_g[l][None, :], cln_b[l][None, :],
                              _pick_tile(N, 256))
            ctx = _merge(zc, oc_na, gnc, oc_wa, gwc, mgc, ctx, mod_c, wa_b, wb_b, wc_b, wo_b, tm_c)
        x = x_new
    return x
```

```python
import functools

import numpy as np
import jax
import jax.numpy as jnp
from jax import lax
from jax.experimental import pallas as pl
from jax.experimental.pallas import tpu as pltpu

D_MODEL = 1024
GRID_W = 64
HEAD_DIM = 64
D_CONV = 512
CONV_K = 31
NA_HEADS = 8
NA_MAX_ROWS = 8
NA_COLS = 16
WA_HEADS = 8
WA_KV_HEADS = 2
WA_GROUP = WA_HEADS // WA_KV_HEADS
WA_WINDOW = 128
WA_BLOCK = 128
ROPE_BASE = 10000.0
EPS = 1e-6
D_NA = NA_HEADS * HEAD_DIM
D_WA = WA_HEADS * HEAD_DIM
D_WA_KV = WA_KV_HEADS * HEAD_DIM
SPLIT_SIZES = (2 * D_CONV, D_CONV, D_NA, D_NA, D_NA, D_NA, D_WA, D_WA_KV, D_WA_KV, D_WA, 3 * D_MODEL)
D_IN = sum(SPLIT_SIZES)
SPLIT_STARTS = tuple(int(v) for v in np.cumsum((0,) + SPLIT_SIZES[:-1]))

LANES = 128
MXU_DIM = 256
HEAD_PAIR = 2 * HEAD_DIM
ROPE_HALF = HEAD_DIM // 4
NEG_BIG = -1e30
LOG2E = 1.4426950408889634
VMEM_LIMIT = 56 * 1024 * 1024

F32 = jnp.float32
BF16 = jnp.bfloat16


def _sigmoid(v):
    return 1.0 / (1.0 + jnp.exp(-v))


def _silu(v):
    return v * _sigmoid(v)


def _params(n_grid):
    return pltpu.CompilerParams(dimension_semantics=("arbitrary",) * n_grid,
                                vmem_limit_bytes=VMEM_LIMIT)


def _mod_kernel(c_ref, w_ref, b_ref, o_ref):
    s = _silu(c_ref[...])
    o_ref[...] = jnp.dot(s.astype(BF16), w_ref[...].astype(BF16),
                         preferred_element_type=F32) + b_ref[...]


def _modulation(c_all, w_ada, b_ada):
    rows = c_all.shape[0]
    n_out = w_ada.shape[1]
    blk = D_MODEL
    return pl.pallas_call(
        _mod_kernel,
        grid=(n_out // blk,),
        in_specs=[pl.BlockSpec((rows, D_MODEL), lambda j: (0, 0)),
                  pl.BlockSpec((D_MODEL, blk), lambda j: (0, j)),
                  pl.BlockSpec((1, blk), lambda j: (0, j))],
        out_specs=pl.BlockSpec((rows, blk), lambda j: (0, j)),
        out_shape=jax.ShapeDtypeStruct((rows, n_out), F32),
        compiler_params=_params(1),
        name="modulation",
    )(c_all, w_ada, b_ada)


D_PROJ = SPLIT_STARTS[10]


def _modulated_norm(x, mod_ref, g_ref):
    ms = jnp.mean(x * x, axis=-1, keepdims=True)
    shift = mod_ref[0, :, 0:D_MODEL]
    scale = mod_ref[0, :, D_MODEL:2 * D_MODEL]
    return ((x * lax.rsqrt(ms + EPS) * g_ref[...]) * (1.0 + scale) + shift).astype(BF16)


def _proj_kernel(*refs, rope, kv_only):
    if rope:
        (x_ref, mod_ref, g_ref, w_ref, nqw_ref, nkw_ref, wqw_ref, wkw_ref, bd_ref,
         cos_ref, s1_ref, s2_ref) = refs[:12]
        outs = refs[12:]
    else:
        (x_ref, mod_ref, g_ref, w_ref, nqw_ref, nkw_ref, wqw_ref, wkw_ref, bd_ref) = refs[:9]
        cos_ref = s1_ref = s2_ref = None
        outs = refs[9:]
    if kv_only:
        nk_ref, nv_ref, wk_ref, wv_ref, h_scr = outs
    else:
        (a_ref, ga_ref, nq_ref, nk_ref, nv_ref, gn_ref, wq_ref, wk_ref, wv_ref, gw_ref,
         h_scr) = outs

    h_scr[...] = _modulated_norm(x_ref[0], mod_ref, g_ref)

    st = SPLIT_STARTS

    def seg(c0, n):
        return jnp.dot(h_scr[...], w_ref[:, c0:c0 + n], preferred_element_type=F32)

    def head_norm(t, w_row_ref):
        n = t.shape[1]
        sq = (t * t).astype(BF16)
        if n >= MXU_DIM:
            parts = [jnp.dot(sq[:, c:c + MXU_DIM], bd_ref[...], preferred_element_type=F32)
                     for c in range(0, n, MXU_DIM)]
            ss = jnp.concatenate(parts, axis=1) if len(parts) > 1 else parts[0]
        else:
            ss = jnp.dot(sq, bd_ref[0:n, 0:n], preferred_element_type=F32)
        return t * lax.rsqrt(ss * (1.0 / HEAD_DIM) + EPS) * w_row_ref[...]

    def rotary(tc):
        if not rope:
            return tc
        lo = pltpu.roll(tc, ROPE_HALF, axis=1)
        hi = pltpu.roll(tc, LANES - ROPE_HALF, axis=1)
        return tc * cos_ref[...] + lo * s1_ref[...] + hi * s2_ref[...]

    def store_kv_dup(t, o_ref):
        sw = pltpu.roll(t, HEAD_DIM, axis=1)
        low = _low_half(t.shape[0])
        o_ref[0, :, 0:LANES] = jnp.where(low, t, sw).astype(BF16)
        o_ref[0, :, LANES:2 * LANES] = jnp.where(low, sw, t).astype(BF16)

    nk_ref[0] = head_norm(seg(st[3], D_NA), nkw_ref).astype(BF16)
    nv_ref[0] = seg(st[4], D_NA).astype(BF16)
    store_kv_dup(rotary(head_norm(seg(st[7], D_WA_KV), wkw_ref)), wk_ref)
    store_kv_dup(seg(st[8], D_WA_KV), wv_ref)
    if kv_only:
        return
    glu = seg(st[0], 2 * D_CONV)
    a_ref[0] = (glu[:, :D_CONV] * _sigmoid(glu[:, D_CONV:])).astype(BF16)
    ga_ref[0] = _silu(seg(st[1], D_CONV)).astype(BF16)
    nq_ref[0] = head_norm(seg(st[2], D_NA), nqw_ref).astype(BF16)
    gn_ref[0] = _silu(seg(st[5], D_NA)).astype(BF16)
    wq = head_norm(seg(st[6], D_WA), wqw_ref)
    for c in range(0, D_WA, LANES):
        wq_ref[0, :, c:c + LANES] = rotary(wq[:, c:c + LANES]).astype(BF16)
    gw_ref[0] = _silu(seg(st[9], D_WA)).astype(BF16)


_PROJ_OUT_WIDTHS = (D_CONV, D_CONV, D_NA, D_NA, D_NA, D_NA, D_WA, 2 * D_WA_KV, 2 * D_WA_KV, D_WA)
_PROJ_KV_WIDTHS = (D_NA, D_NA, 2 * D_WA_KV, 2 * D_WA_KV)


def _projection(xs, mod, norm_g, w_in, nqw, nkw, wqw, wkw, bd, rope_tabs, tm, kv_only=False):
    B, T, _ = xs.shape
    rope = rope_tabs is not None
    widths = _PROJ_KV_WIDTHS if kv_only else _PROJ_OUT_WIDTHS
    const2 = lambda b, t: (0, 0)
    row_spec = lambda n: pl.BlockSpec((1, n), const2)
    in_specs = [
        pl.BlockSpec((1, tm, D_MODEL), lambda b, t: (b, t, 0)),
        pl.BlockSpec((1, 1, 3 * D_MODEL), lambda b, t: (b, 0, 0)),
        row_spec(D_MODEL),
        pl.BlockSpec((D_MODEL, D_PROJ), const2, pipeline_mode=pl.Buffered(1)),
        row_spec(D_NA), row_spec(D_NA), row_spec(D_WA), row_spec(D_WA_KV),
        pl.BlockSpec((MXU_DIM, MXU_DIM), const2),
    ]
    args = [xs, mod, norm_g, w_in, nqw, nkw, wqw, wkw, bd]
    if rope:
        in_specs += [pl.BlockSpec((tm, LANES), lambda b, t: (t, 0))] * 3
        args += list(rope_tabs)
    out_specs = [pl.BlockSpec((1, tm, n), lambda b, t: (b, t, 0)) for n in widths]
    out_shape = [jax.ShapeDtypeStruct((B, T, n), BF16) for n in widths]
    return pl.pallas_call(
        functools.partial(_proj_kernel, rope=rope, kv_only=kv_only),
        grid=(B, T // tm),
        in_specs=in_specs,
        out_specs=out_specs,
        out_shape=out_shape,
        scratch_shapes=[pltpu.VMEM((tm, D_MODEL), BF16)],
        compiler_params=_params(2),
        name="projection_rope" if rope else ("projection_ctx_kv" if kv_only else "projection_ctx"),
    )(*args)


def _nt_dot(a, b):
    return lax.dot_general(a, b, (((1,), (1,)), ((), ())), preferred_element_type=F32)


def _low_half(rows):
    return lax.broadcasted_iota(jnp.int32, (rows, HEAD_PAIR), 1) < HEAD_DIM


def _keep_half(q_pair, half):
    lane = lax.broadcasted_iota(jnp.int32, (1, HEAD_PAIR), 1)
    keep = (lane < HEAD_DIM) if half == 0 else (lane >= HEAD_DIM)
    return q_pair * keep.astype(F32).astype(q_pair.dtype)


def _softmax_pv(scores, values, sinks=None):
    rows = scores.shape[0]
    n_chunks = len(sinks) if sinks else 1
    chunk = rows // n_chunks
    p_chunks, inv_chunks = [], []
    for c in range(n_chunks):
        s = scores[c * chunk:(c + 1) * chunk]
        m = jnp.max(s, axis=-1, keepdims=True)
        if sinks:
            m = jnp.maximum(m, sinks[c])
        p = jnp.exp2(s - m)
        den = jnp.sum(p, axis=-1, keepdims=True)
        if sinks:
            den = den + jnp.exp2(sinks[c] - m)
        p_chunks.append(p.astype(BF16))
        inv_chunks.append(1.0 / den)
    p = p_chunks[0] if n_chunks == 1 else jnp.concatenate(p_chunks, axis=0)
    inv = inv_chunks[0] if n_chunks == 1 else jnp.concatenate(inv_chunks, axis=0)
    return jnp.dot(p, values, preferred_element_type=F32), inv


PIPELINE_AHEAD = 1


def _pipelined(problems, scores_fn, finish_fn):
    pending = [scores_fn(*p) for p in problems[:PIPELINE_AHEAD]]
    for i, prob in enumerate(problems):
        if i + PIPELINE_AHEAD < len(problems):
            pending.append(scores_fn(*problems[i + PIPELINE_AHEAD]))
        finish_fn(*prob, pending.pop(0))


NA_QROWS = 4
NA_KROWS = NA_QROWS + NA_MAX_ROWS


def _na_kernel(q_ref, k_ref, v_ref, kc_ref, vc_ref, bias_ref, o_ref, *, n_rows):
    t = pl.program_id(1)
    tq = NA_QROWS * GRID_W
    n_win = NA_KROWS * GRID_W
    kb = jnp.clip(t * NA_QROWS - NA_MAX_ROWS // 2, 0, n_rows - NA_KROWS)
    k0 = pl.multiple_of(kb * GRID_W, GRID_W)
    low = _low_half(tq)

    held = {}

    def scores(hp, half):
        lanes = slice(hp * HEAD_PAIR, (hp + 1) * HEAD_PAIR)
        q_half = _keep_half(q_ref[0, :, lanes], half)
        keys = jnp.concatenate([k_ref[0, pl.ds(k0, n_win), lanes], kc_ref[0, :, lanes]], axis=0)
        s = _nt_dot(q_half, keys)
        return jnp.concatenate([s[:, :n_win] + bias_ref[0, hp, half * tq:(half + 1) * tq], s[:, n_win:]], axis=1)

    def finish(hp, half, sc):
        lanes = slice(hp * HEAD_PAIR, (hp + 1) * HEAD_PAIR)
        vals = jnp.concatenate([v_ref[0, pl.ds(k0, n_win), lanes], vc_ref[0, :, lanes]], axis=0)
        o, inv = _softmax_pv(sc, vals)
        o = o * inv
        if half == 0:
            held[hp] = o
        else:
            o_ref[0, :, lanes] = jnp.where(low, held.pop(hp), o).astype(BF16)

    _pipelined([(hp, half) for hp in range(NA_HEADS // 2) for half in range(2)], scores, finish)


def _na_attention(q, k, v, kc, vc, bias):
    B, S, _ = q.shape
    N = kc.shape[1]
    n_rows = S // GRID_W
    assert S % GRID_W == 0 and n_rows % NA_QROWS == 0 and n_rows >= NA_KROWS
    n_steps = n_rows // NA_QROWS
    tq = NA_QROWS * GRID_W
    full = lambda b, t: (b, 0, 0)
    variant = lambda b, t: (jnp.where(t == 0, 0, jnp.where(t == n_steps - 1, 2, 1)), 0, 0, 0)
    return pl.pallas_call(
        functools.partial(_na_kernel, n_rows=n_rows),
        grid=(B, n_steps),
        in_specs=[pl.BlockSpec((1, tq, D_NA), lambda b, t: (b, t, 0)),
                  pl.BlockSpec((1, S, D_NA), full),
                  pl.BlockSpec((1, S, D_NA), full),
                  pl.BlockSpec((1, N, D_NA), full),
                  pl.BlockSpec((1, N, D_NA), full),
                  pl.BlockSpec((1,) + bias.shape[1:], variant)],
        out_specs=pl.BlockSpec((1, tq, D_NA), lambda b, t: (b, t, 0)),
        out_shape=jax.ShapeDtypeStruct((B, S, D_NA), BF16),
        compiler_params=_params(2),
        name="na_attention",
    )(q, k, v, kc, vc, bias)


def _na_bias_table(rpb, n_rows):
    j = np.arange(GRID_W)[:, None]
    kc = np.arange(GRID_W)[None, :]
    cs = np.clip(j - NA_COLS // 2, 0, GRID_W - NA_COLS)
    col_ok = (kc >= cs) & (kc < cs + NA_COLS)
    c_off = kc - j + (NA_COLS - 1)
    pick = (np.arange(2 * NA_COLS - 1)[:, None, None] == c_off[None]).astype(np.float32)
    by_off = jnp.einsum("hrc,cjk->hjrk", rpb.astype(F32) * LOG2E, pick, precision=lax.Precision.HIGHEST)
    by_off = jnp.where(col_ok[None, :, None, :], by_off, NEG_BIG)
    by_off = by_off.reshape(NA_HEADS, GRID_W, (2 * NA_MAX_ROWS - 1) * GRID_W)
    tabs = []
    for i0 in (0, NA_QROWS, n_rows - NA_QROWS):
        kb = int(np.clip(i0 - NA_MAX_ROWS // 2, 0, n_rows - NA_KROWS))
        per_row = []
        for i in range(i0, i0 + NA_QROWS):
            rs = int(np.clip(i - NA_MAX_ROWS // 2, 0, n_rows - NA_MAX_ROWS))
            kr_lo, off_lo = rs - kb, rs - i + (NA_MAX_ROWS - 1)
            blk = by_off[:, :, off_lo * GRID_W:(off_lo + NA_MAX_ROWS) * GRID_W]
            per_row.append(jnp.pad(
                blk, ((0, 0), (0, 0), (kr_lo * GRID_W, (NA_KROWS - NA_MAX_ROWS - kr_lo) * GRID_W)),
                constant_values=NEG_BIG))
        tabs.append(jnp.stack(per_row, axis=1).reshape(
            NA_HEADS // 2, 2 * NA_QROWS * GRID_W, NA_KROWS * GRID_W))
    return jnp.stack(tabs)


WA_QBLOCKS = 2


def _wa_kernel(sink_ref, q_ref, k_ref, v_ref, kc_ref, vc_ref, o_ref, *, seq):
    step = pl.program_id(1)
    span = WA_BLOCK + 2 * WA_WINDOW
    low = _low_half(WA_BLOCK)
    row = lax.broadcasted_iota(jnp.int32, (WA_BLOCK, span), 0)
    col = lax.broadcasted_iota(jnp.int32, (WA_BLOCK, span), 1)
    key0, mask_add = [], []
    for sb in range(WA_QBLOCKS):
        n = step * WA_QBLOCKS + sb
        ks = jnp.clip(n * WA_BLOCK - WA_WINDOW, 0, seq - span)
        rel = col + (ks - n * WA_BLOCK) - row
        m = jnp.where(jnp.abs(rel) <= WA_WINDOW, 0.0, NEG_BIG)
        key0.append(pl.multiple_of(ks, WA_BLOCK))
        mask_add.append(jnp.concatenate([m, m], axis=0))

    def kv_lanes(pair):
        kh = pair * 2 // WA_GROUP
        return slice(kh * HEAD_PAIR, (kh + 1) * HEAD_PAIR)

    def scores(sb, pair):
        q_pair = q_ref[0, sb * WA_BLOCK:(sb + 1) * WA_BLOCK, pair * HEAD_PAIR:(pair + 1) * HEAD_PAIR]
        q_stack = jnp.concatenate([_keep_half(q_pair, 0), _keep_half(q_pair, 1)], axis=0)
        keys = jnp.concatenate(
            [k_ref[0, pl.ds(key0[sb], span), kv_lanes(pair)], kc_ref[0, :, kv_lanes(pair)]], axis=0)
        s = _nt_dot(q_stack, keys)
        return jnp.concatenate([s[:, :span] + mask_add[sb], s[:, span:]], axis=1)

    def finish(sb, pair, sc):
        vals = jnp.concatenate(
            [v_ref[0, pl.ds(key0[sb], span), kv_lanes(pair)], vc_ref[0, :, kv_lanes(pair)]], axis=0)
        sinks = [sink_ref[2 * pair + half] * LOG2E for half in range(2)]
        o, inv = _softmax_pv(sc, vals, sinks)
        o = o * inv
        o_ref[0, sb * WA_BLOCK:(sb + 1) * WA_BLOCK, pair * HEAD_PAIR:(pair + 1) * HEAD_PAIR] = jnp.where(
            low, o[:WA_BLOCK], o[WA_BLOCK:]).astype(BF16)

    _pipelined([(sb, pair) for sb in range(WA_QBLOCKS) for pair in range(WA_HEADS // 2)], scores, finish)


def _wa_attention(q, k, v, kc, vc, sink):
    B, S, _ = q.shape
    N = kc.shape[1]
    tq = WA_QBLOCKS * WA_BLOCK
    assert S % tq == 0 and S >= WA_BLOCK + 2 * WA_WINDOW
    full = lambda b, t: (b, 0, 0)
    return pl.pallas_call(
        functools.partial(_wa_kernel, seq=S),
        grid=(B, S // tq),
        in_specs=[pl.BlockSpec(memory_space=pltpu.SMEM),
                  pl.BlockSpec((1, tq, D_WA), lambda b, t: (b, t, 0)),
                  pl.BlockSpec((1, S, 2 * D_WA_KV), full),
                  pl.BlockSpec((1, S, 2 * D_WA_KV), full),
                  pl.BlockSpec((1, N, 2 * D_WA_KV), full),
                  pl.BlockSpec((1, N, 2 * D_WA_KV), full)],
        out_specs=pl.BlockSpec((1, tq, D_WA), lambda b, t: (b, t, 0)),
        out_shape=jax.ShapeDtypeStruct((B, S, D_WA), BF16),
        compiler_params=_params(2),
        name="wa_attention",
    )(sink, q, k, v, kc, vc)


def _ctx_attn_kernel(sink_ref, nq_ref, nk_ref, nv_ref, wq_ref, wk_ref, wv_ref, ona_ref, owa_ref):
    n_ctx = nq_ref.shape[1]
    low = _low_half(n_ctx)
    for hp in range(NA_HEADS // 2):
        lanes = slice(hp * HEAD_PAIR, (hp + 1) * HEAD_PAIR)
        q_pair = nq_ref[0, :, lanes]
        k_c = nk_ref[0, :, lanes]
        v_c = nv_ref[0, :, lanes]
        halves = []
        for half in range(2):
            qh = _keep_half(q_pair, half)
            o, inv = _softmax_pv(_nt_dot(qh, k_c), v_c)
            halves.append(o * inv)
        ona_ref[0, :, lanes] = jnp.where(low, halves[0], halves[1]).astype(BF16)
    for pair in range(WA_HEADS // 2):
        lanes = slice(pair * HEAD_PAIR, (pair + 1) * HEAD_PAIR)
        kh = pair * 2 // WA_GROUP
        kv_lanes = slice(kh * HEAD_PAIR, (kh + 1) * HEAD_PAIR)
        q_pair = wq_ref[0, :, lanes]
        k_c = wk_ref[0, :, kv_lanes]
        v_c = wv_ref[0, :, kv_lanes]
        halves = []
        for half in range(2):
            qh = _keep_half(q_pair, half)
            o, inv = _softmax_pv(_nt_dot(qh, k_c), v_c, [sink_ref[2 * pair + half] * LOG2E])
            halves.append(o * inv)
        owa_ref[0, :, lanes] = jnp.where(low, halves[0], halves[1]).astype(BF16)


def _ctx_attention(nq, nk, nv, wq, wk, wv, sink):
    B, N, _ = nq.shape
    blk = lambda n: pl.BlockSpec((1, N, n), lambda b: (b, 0, 0))
    return pl.pallas_call(
        _ctx_attn_kernel,
        grid=(B,),
        in_specs=[pl.BlockSpec(memory_space=pltpu.SMEM),
                  blk(D_NA), blk(D_NA), blk(D_NA), blk(D_WA), blk(2 * D_WA_KV), blk(2 * D_WA_KV)],
        out_specs=[blk(D_NA), blk(D_WA)],
        out_shape=[jax.ShapeDtypeStruct((B, N, D_NA), BF16), jax.ShapeDtypeStruct((B, N, D_WA), BF16)],
        compiler_params=_params(1),
        name="ctx_attention",
    )(sink, nq, nk, nv, wq, wk, wv)


CONV_HALO = 16
SUBLANES = 8
CONV_CHUNK = 64


def _conv_kernel(a_ref, ga_ref, w_ref, cb_ref, lg_ref, lb_ref, z_ref, pad_scr, shift_scr, y_scr,
                 *, tm, n_tiles):
    j = pl.program_id(1)
    t0 = pl.multiple_of(j * tm, tm)
    pad_scr[CONV_HALO:CONV_HALO + tm, :] = a_ref[0, pl.ds(t0, tm), :].astype(F32)
    zeros = jnp.zeros((CONV_HALO, D_CONV), F32)

    @pl.when(j > 0)
    def _():
        pad_scr[0:CONV_HALO, :] = a_ref[0, pl.ds(t0 - CONV_HALO, CONV_HALO), :].astype(F32)

    @pl.when(j == 0)
    def _():
        pad_scr[0:CONV_HALO, :] = zeros

    @pl.when(j < n_tiles - 1)
    def _():
        pad_scr[CONV_HALO + tm:, :] = a_ref[0, pl.ds(t0 + tm, CONV_HALO), :].astype(F32)

    @pl.when(j == n_tiles - 1)
    def _():
        pad_scr[CONV_HALO + tm:, :] = zeros

    n_shift = tm + CONV_HALO + SUBLANES
    for r in range(1, SUBLANES):
        shift_scr[r - 1] = pad_scr[r:r + n_shift, :]

    def tap(k, row0, lanes):
        r, base = (k + 1) % SUBLANES, (k + 1) // SUBLANES * SUBLANES
        if r == 0:
            return pad_scr[pl.ds(row0 + base, SUBLANES), lanes]
        return shift_scr[r - 1, pl.ds(row0 + base, SUBLANES), lanes]

    groups = CONV_CHUNK // SUBLANES
    for c in range(D_CONV // LANES):
        lanes = slice(c * LANES, (c + 1) * LANES)
        b_rows = jnp.broadcast_to(cb_ref[:, lanes], (SUBLANES, LANES))

        def chunk(g, carry, lanes=lanes, b_rows=b_rows):
            row0 = pl.multiple_of(g * CONV_CHUNK, CONV_CHUNK)
            sums = [[b_rows, None] for _ in range(groups)]
            for k in range(CONV_K):
                w_row = jnp.broadcast_to(w_ref[k:k + 1, lanes], (SUBLANES, LANES))
                for q in range(groups):
                    term = tap(k, row0 + q * SUBLANES, lanes) * w_row
                    prev = sums[q][k % 2]
                    sums[q][k % 2] = term if prev is None else prev + term
            for q in range(groups):
                y_scr[pl.ds(row0 + q * SUBLANES, SUBLANES), lanes] = sums[q][0] + sums[q][1]
            return carry

        lax.fori_loop(0, tm // CONV_CHUNK, chunk, 0)

    acc = y_scr[...]
    mu = jnp.mean(acc, axis=-1, keepdims=True)
    d = acc - mu
    var = jnp.mean(d * d, axis=-1, keepdims=True)
    y = d * lax.rsqrt(var + EPS) * lg_ref[...] + lb_ref[...]
    z_ref[0] = (_silu(y) * ga_ref[0].astype(F32)).astype(BF16)


def _conv_branch(a, ga, conv_w, conv_b, ln_g, ln_b, tm):
    B, T, _ = a.shape
    n_tiles = T // tm
    const2 = lambda b, t: (0, 0)
    row = pl.BlockSpec((1, D_CONV), const2)
    return pl.pallas_call(
        functools.partial(_conv_kernel, tm=tm, n_tiles=n_tiles),
        grid=(B, n_tiles),
        in_specs=[pl.BlockSpec((1, T, D_CONV), lambda b, t: (b, 0, 0)),
                  pl.BlockSpec((1, tm, D_CONV), lambda b, t: (b, t, 0)),
                  pl.BlockSpec((CONV_K, D_CONV), const2),
                  row, row, row],
        out_specs=pl.BlockSpec((1, tm, D_CONV), lambda b, t: (b, t, 0)),
        out_shape=jax.ShapeDtypeStruct((B, T, D_CONV), BF16),
        scratch_shapes=[pltpu.VMEM((tm + 2 * CONV_HALO, D_CONV), F32),
                        pltpu.VMEM((SUBLANES - 1, tm + CONV_HALO + SUBLANES, D_CONV), F32),
                        pltpu.VMEM((tm, D_CONV), F32)],
        compiler_params=_params(2),
        name="conv_branch",
    )(a, ga, conv_w, conv_b, ln_g, ln_b)


def _merge_kernel(z_ref, ona_ref, gn_ref, owa_ref, gw_ref, x_ref, mod_ref, g_ref,
                  wmg_ref, wa_ref, wb_ref, wc_ref, wo_ref, o_ref):
    def gated(o_r, g_r):
        return (o_r[0].astype(F32) * g_r[0].astype(F32)).astype(BF16)

    x = x_ref[0]
    h = _modulated_norm(x, mod_ref, g_ref)
    branches = (jnp.dot(z_ref[0], wa_ref[...], preferred_element_type=F32),
                jnp.dot(gated(ona_ref, gn_ref), wb_ref[...], preferred_element_type=F32),
                jnp.dot(gated(owa_ref, gw_ref), wc_ref[...], preferred_element_type=F32))
    y = None
    for i, yi in enumerate(branches):
        gate_i = _sigmoid(jnp.dot(h, wmg_ref[:, i * D_MODEL:(i + 1) * D_MODEL], preferred_element_type=F32))
        y = gate_i * yi if y is None else y + gate_i * yi
    gate = mod_ref[0, :, 2 * D_MODEL:]
    o_ref[0] = x + gate * jnp.dot(y.astype(BF16), wo_ref[...], preferred_element_type=F32)


def _merge(z, ona, gn, owa, gw, xs, mod, norm_g, wmg, wa, wb, wc, wo, tm):
    B, T, _ = xs.shape
    tok = lambda n: pl.BlockSpec((1, tm, n), lambda b, t: (b, t, 0))
    const2 = lambda b, t: (0, 0)
    once = lambda r, c: pl.BlockSpec((r, c), const2, pipeline_mode=pl.Buffered(1))
    return pl.pallas_call(
        _merge_kernel,
        grid=(B, T // tm),
        in_specs=[tok(D_CONV), tok(D_NA), tok(D_NA), tok(D_WA), tok(D_WA), tok(D_MODEL),
                  pl.BlockSpec((1, 1, 3 * D_MODEL), lambda b, t: (b, 0, 0)),
                  pl.BlockSpec((1, D_MODEL), const2),
                  once(D_MODEL, 3 * D_MODEL), once(D_CONV, D_MODEL), once(D_NA, D_MODEL),
                  once(D_WA, D_MODEL), once(D_MODEL, D_MODEL)],
        out_specs=tok(D_MODEL),
        out_shape=jax.ShapeDtypeStruct((B, T, D_MODEL), F32),
        compiler_params=_params(2),
        name="merge",
    )(z, ona, gn, owa, gw, xs, mod, norm_g, wmg, wa, wb, wc, wo)


def _rope_tables(seq):
    t = np.arange(seq)
    half = HEAD_DIM // 4
    inv = ROPE_BASE ** (-np.arange(half, dtype=np.float32) / half)
    lane = np.arange(LANES)
    d = lane % HEAD_DIM
    use_col = (d // (HEAD_DIM // 2)) == 1
    dd = d % (HEAD_DIM // 2)
    first = dd < half
    pos = jnp.where(use_col[None, :], (t % GRID_W)[:, None], (t // GRID_W)[:, None]).astype(F32)
    ang = pos * jnp.asarray(inv[dd % half], F32)[None, :]
    cos, sin = jnp.cos(ang), jnp.sin(ang)
    s1 = jnp.where(first[None, :], 0.0, sin)
    s2 = jnp.where(first[None, :], -sin, 0.0)
    return cos, s1, s2


def _block_diag_ones():
    i = np.arange(MXU_DIM)
    return jnp.asarray((i[:, None] // HEAD_DIM) == (i[None, :] // HEAD_DIM), BF16)


def _pick_tile(n, target):
    t = min(n, target)
    while n % t:
        t //= 2
    return t


def kernel(x, c, ctx, c_ctx, norm_g, w_ada, b_ada, w_in, conv_w, conv_b, cln_g, cln_b, w_proj_a,
           na_q_norm, na_k_norm, na_rpb, w_proj_b, wa_q_norm, wa_k_norm, wa_sink, w_proj_c, w_o):
    B, S, _ = x.shape
    N = ctx.shape[1]
    depth = w_in.shape[0]
    assert B <= 8
    c_all = jnp.zeros((16, D_MODEL), F32).at[:B].set(c).at[B].set(c_ctx)
    rope_tabs = _rope_tables(S)
    bd = _block_diag_ones()
    q_scale = HEAD_DIM ** -0.5 * LOG2E
    tm_x = _pick_tile(S, 512)
    tm_c = _pick_tile(N, 512)
    tile_row = lambda w, n: jnp.tile(w.astype(F32), n)[None, :]

    for l in range(depth):
        update_ctx = l < depth - 1
        mod = _modulation(c_all, w_ada[l], b_ada[l][None, :])
        mod_x = mod[:B, None, :]
        mod_c = jnp.broadcast_to(mod[B][None, None, :], (B, 1, 3 * D_MODEL))

        w_l = w_in[l][:, :D_PROJ].astype(BF16)
        w_mg = w_in[l][:, D_PROJ:].astype(BF16)
        nqw = tile_row(na_q_norm[l], NA_HEADS) * q_scale
        nkw = tile_row(na_k_norm[l], NA_HEADS)
        wqw = tile_row(wa_q_norm[l], WA_HEADS) * q_scale
        wkw = tile_row(wa_k_norm[l], WA_KV_HEADS)
        g_row = norm_g[l][None, :]

        (a, ga, nq, nk, nv, gn, wq, wk, wv, gw) = _projection(
            x, mod_x, g_row, w_l, nqw, nkw, wqw, wkw, bd, rope_tabs, tm_x)
        if update_ctx:
            (ac, gac, ncq, nkc, nvc, gnc, wcq, wkc, wvc, gwc) = _projection(
                ctx, mod_c, g_row, w_l, nqw, nkw, wqw, wkw, bd, None, tm_c)
        else:
            nkc, nvc, wkc, wvc = _projection(
                ctx, mod_c, g_row, w_l, nqw, nkw, wqw, wkw, bd, None, tm_c, kv_only=True)

        bias = _na_bias_table(na_rpb[l], S // GRID_W)
        sink = wa_sink[l].astype(F32)
        o_na = _na_attention(nq, nk, nv, nkc, nvc, bias)
        o_wa = _wa_attention(wq, wk, wv, wkc, wvc, sink)
        z = _conv_branch(a, ga, conv_w[l], conv_b[l][None, :], cln_g[l][None, :], cln_b[l][None, :],
                         _pick_tile(S, 256))
        wa_b = w_proj_a[l].astype(BF16)
        wb_b = w_proj_b[l].astype(BF16)
        wc_b = w_proj_c[l].astype(BF16)
        wo_b = w_o[l].astype(BF16)
        x_new = _merge(z, o_na, gn, o_wa, gw, x, mod_x, g_row, w_mg, wa_b, wb_b, wc_b, wo_b, tm_x)

        if update_ctx:
            oc_na, oc_wa = _ctx_attention(ncq, nkc, nvc, wcq, wkc, wvc, sink)
            zc = _conv_branch(ac, gac, conv_w[l], conv_b[l][None, :], cln_g[l][None, :], cln_b[l][None, :],
                              _pick_tile(N, 256))
            ctx = _merge(zc, oc_na, gnc, oc_wa, gwc, ctx, mod_c, g_row, w_mg, wa_b, wb_b, wc_b, wo_b, tm_c)
        x = x_new
    return x
```

```python
import functools

import numpy as np
import jax
import jax.numpy as jnp
from jax import lax
from jax.experimental import pallas as pl
from jax.experimental.pallas import tpu as pltpu

D_MODEL = 1024
GRID_W = 64
HEAD_DIM = 64
D_CONV = 512
CONV_K = 31
NA_HEADS = 8
NA_MAX_ROWS = 8
NA_COLS = 16
WA_HEADS = 8
WA_KV_HEADS = 2
WA_GROUP = WA_HEADS // WA_KV_HEADS
WA_WINDOW = 128
WA_BLOCK = 128
ROPE_BASE = 10000.0
EPS = 1e-6
D_NA = NA_HEADS * HEAD_DIM
D_WA = WA_HEADS * HEAD_DIM
D_WA_KV = WA_KV_HEADS * HEAD_DIM
SPLIT_SIZES = (2 * D_CONV, D_CONV, D_NA, D_NA, D_NA, D_NA, D_WA, D_WA_KV, D_WA_KV, D_WA, 3 * D_MODEL)
D_IN = sum(SPLIT_SIZES)
SPLIT_STARTS = tuple(int(v) for v in np.cumsum((0,) + SPLIT_SIZES[:-1]))

LANES = 128
MXU_DIM = 256
HEAD_PAIR = 2 * HEAD_DIM
ROPE_HALF = HEAD_DIM // 4
NEG_BIG = -1e30
LOG2E = 1.4426950408889634
VMEM_LIMIT = 56 * 1024 * 1024

F32 = jnp.float32
BF16 = jnp.bfloat16


def _sigmoid(v):
    return 1.0 / (1.0 + jnp.exp(-v))


def _silu(v):
    return v * _sigmoid(v)


def _params(n_grid):
    return pltpu.CompilerParams(dimension_semantics=("arbitrary",) * n_grid,
                                vmem_limit_bytes=VMEM_LIMIT)


def _mod_kernel(c_ref, w_ref, b_ref, o_ref):
    s = _silu(c_ref[...])
    o_ref[...] = jnp.dot(s.astype(BF16), w_ref[...].astype(BF16),
                         preferred_element_type=F32) + b_ref[...]


def _modulation(c_all, w_ada, b_ada):
    rows = c_all.shape[0]
    n_out = w_ada.shape[1]
    blk = D_MODEL
    return pl.pallas_call(
        _mod_kernel,
        grid=(n_out // blk,),
        in_specs=[pl.BlockSpec((rows, D_MODEL), lambda j: (0, 0)),
                  pl.BlockSpec((D_MODEL, blk), lambda j: (0, j)),
                  pl.BlockSpec((1, blk), lambda j: (0, j))],
        out_specs=pl.BlockSpec((rows, blk), lambda j: (0, j)),
        out_shape=jax.ShapeDtypeStruct((rows, n_out), F32),
        compiler_params=_params(1),
        name="modulation",
    )(c_all, w_ada, b_ada)


PROJ_SUB = 512
D_PROJ = SPLIT_STARTS[10]


def _modulated_norm(x, mod_ref, g_ref):
    ms = jnp.mean(x * x, axis=-1, keepdims=True)
    shift = mod_ref[0, :, 0:D_MODEL]
    scale = mod_ref[0, :, D_MODEL:2 * D_MODEL]
    return ((x * lax.rsqrt(ms + EPS) * g_ref[...]) * (1.0 + scale) + shift).astype(BF16)


def _proj_kernel(*refs, rope, kv_only):
    if rope:
        (x_ref, mod_ref, g_ref, w_ref, nqw_ref, nkw_ref, wqw_ref, wkw_ref, bd_ref,
         cos_ref, s1_ref, s2_ref) = refs[:12]
        outs = refs[12:]
    else:
        (x_ref, mod_ref, g_ref, w_ref, nqw_ref, nkw_ref, wqw_ref, wkw_ref, bd_ref) = refs[:9]
        cos_ref = s1_ref = s2_ref = None
        outs = refs[9:]
    if kv_only:
        nk_ref, nv_ref, wk_ref, wv_ref, h_scr = outs
    else:
        (a_ref, ga_ref, nq_ref, nk_ref, nv_ref, gn_ref, wq_ref, wk_ref, wv_ref, gw_ref,
         h_scr) = outs

    st = SPLIT_STARTS
    tm = x_ref.shape[1]
    sub = min(tm, PROJ_SUB)
    for r0 in range(0, tm, sub):
        h_scr[r0:r0 + sub, :] = _modulated_norm(x_ref[0, r0:r0 + sub, :], mod_ref, g_ref)

    def head_norm(t, w_row_ref):
        n = t.shape[1]
        sq = (t * t).astype(BF16)
        if n >= MXU_DIM:
            parts = [jnp.dot(sq[:, c:c + MXU_DIM], bd_ref[...], preferred_element_type=F32)
                     for c in range(0, n, MXU_DIM)]
            ss = jnp.concatenate(parts, axis=1) if len(parts) > 1 else parts[0]
        else:
            ss = jnp.dot(sq, bd_ref[0:n, 0:n], preferred_element_type=F32)
        return t * lax.rsqrt(ss * (1.0 / HEAD_DIM) + EPS) * w_row_ref[...]

    low = _low_half(sub)

    for r0 in range(0, tm, sub):
        rows = slice(r0, r0 + sub)

        def seg(c0, n):
            return jnp.dot(h_scr[rows, :], w_ref[:, c0:c0 + n], preferred_element_type=F32)

        def rotary(tc):
            if not rope:
                return tc
            lo = pltpu.roll(tc, ROPE_HALF, axis=1)
            hi = pltpu.roll(tc, LANES - ROPE_HALF, axis=1)
            return tc * cos_ref[rows, :] + lo * s1_ref[rows, :] + hi * s2_ref[rows, :]

        def store_kv_dup(t, o_ref):
            sw = pltpu.roll(t, HEAD_DIM, axis=1)
            o_ref[0, rows, 0:LANES] = jnp.where(low, t, sw).astype(BF16)
            o_ref[0, rows, LANES:2 * LANES] = jnp.where(low, sw, t).astype(BF16)

        nk_ref[0, rows, :] = head_norm(seg(st[3], D_NA), nkw_ref).astype(BF16)
        nv_ref[0, rows, :] = seg(st[4], D_NA).astype(BF16)
        store_kv_dup(rotary(head_norm(seg(st[7], D_WA_KV), wkw_ref)), wk_ref)
        store_kv_dup(seg(st[8], D_WA_KV), wv_ref)
        if kv_only:
            continue
        glu = seg(st[0], 2 * D_CONV)
        a_ref[0, rows, :] = (glu[:, :D_CONV] * _sigmoid(glu[:, D_CONV:])).astype(BF16)
        ga_ref[0, rows, :] = _silu(seg(st[1], D_CONV)).astype(BF16)
        nq_ref[0, rows, :] = head_norm(seg(st[2], D_NA), nqw_ref).astype(BF16)
        gn_ref[0, rows, :] = _silu(seg(st[5], D_NA)).astype(BF16)
        wq = head_norm(seg(st[6], D_WA), wqw_ref)
        for c in range(0, D_WA, LANES):
            wq_ref[0, rows, c:c + LANES] = rotary(wq[:, c:c + LANES]).astype(BF16)
        gw_ref[0, rows, :] = _silu(seg(st[9], D_WA)).astype(BF16)


_PROJ_OUT_WIDTHS = (D_CONV, D_CONV, D_NA, D_NA, D_NA, D_NA, D_WA, 2 * D_WA_KV, 2 * D_WA_KV, D_WA)
_PROJ_KV_WIDTHS = (D_NA, D_NA, 2 * D_WA_KV, 2 * D_WA_KV)


def _projection(xs, mod, norm_g, w_in, nqw, nkw, wqw, wkw, bd, rope_tabs, tm, kv_only=False):
    B, T, _ = xs.shape
    rope = rope_tabs is not None
    widths = _PROJ_KV_WIDTHS if kv_only else _PROJ_OUT_WIDTHS
    const2 = lambda b, t: (0, 0)
    row_spec = lambda n: pl.BlockSpec((1, n), const2)
    in_specs = [
        pl.BlockSpec((1, tm, D_MODEL), lambda b, t: (b, t, 0)),
        pl.BlockSpec((1, 1, 3 * D_MODEL), lambda b, t: (b, 0, 0)),
        row_spec(D_MODEL),
        pl.BlockSpec((D_MODEL, D_PROJ), const2, pipeline_mode=pl.Buffered(1)),
        row_spec(D_NA), row_spec(D_NA), row_spec(D_WA), row_spec(D_WA_KV),
        pl.BlockSpec((MXU_DIM, MXU_DIM), const2),
    ]
    args = [xs, mod, norm_g, w_in, nqw, nkw, wqw, wkw, bd]
    if rope:
        in_specs += [pl.BlockSpec((tm, LANES), lambda b, t: (t, 0))] * 3
        args += list(rope_tabs)
    out_specs = [pl.BlockSpec((1, tm, n), lambda b, t: (b, t, 0)) for n in widths]
    out_shape = [jax.ShapeDtypeStruct((B, T, n), BF16) for n in widths]
    return pl.pallas_call(
        functools.partial(_proj_kernel, rope=rope, kv_only=kv_only),
        grid=(B, T // tm),
        in_specs=in_specs,
        out_specs=out_specs,
        out_shape=out_shape,
        scratch_shapes=[pltpu.VMEM((tm, D_MODEL), BF16)],
        compiler_params=_params(2),
        name="projection_rope" if rope else ("projection_ctx_kv" if kv_only else "projection_ctx"),
    )(*args)


def _nt_dot(a, b):
    return lax.dot_general(a, b, (((1,), (1,)), ((), ())), preferred_element_type=F32)


def _low_half(rows):
    return lax.broadcasted_iota(jnp.int32, (rows, HEAD_PAIR), 1) < HEAD_DIM


def _keep_half(q_pair, half):
    lane = lax.broadcasted_iota(jnp.int32, (1, HEAD_PAIR), 1)
    keep = (lane < HEAD_DIM) if half == 0 else (lane >= HEAD_DIM)
    return q_pair * keep.astype(F32).astype(q_pair.dtype)


def _softmax_pv(scores, values, sinks=None):
    rows = scores.shape[0]
    n_chunks = len(sinks) if sinks else 1
    chunk = rows // n_chunks
    p_chunks, inv_chunks = [], []
    for c in range(n_chunks):
        s = scores[c * chunk:(c + 1) * chunk]
        m = jnp.max(s, axis=-1, keepdims=True)
        if sinks:
            m = jnp.maximum(m, sinks[c])
        p = jnp.exp2(s - m)
        den = jnp.sum(p, axis=-1, keepdims=True)
        if sinks:
            den = den + jnp.exp2(sinks[c] - m)
        p_chunks.append(p.astype(BF16))
        inv_chunks.append(1.0 / den)
    p = p_chunks[0] if n_chunks == 1 else jnp.concatenate(p_chunks, axis=0)
    inv = inv_chunks[0] if n_chunks == 1 else jnp.concatenate(inv_chunks, axis=0)
    return jnp.dot(p, values, preferred_element_type=F32), inv


PIPELINE_AHEAD = 1


def _pipelined(problems, scores_fn, finish_fn):
    pending = [scores_fn(*p) for p in problems[:PIPELINE_AHEAD]]
    for i, prob in enumerate(problems):
        if i + PIPELINE_AHEAD < len(problems):
            pending.append(scores_fn(*problems[i + PIPELINE_AHEAD]))
        finish_fn(*prob, pending.pop(0))


NA_QROWS = 4
NA_KROWS = NA_QROWS + NA_MAX_ROWS


NA_QBLOCKS = 2


def _na_kernel(q_ref, k_ref, v_ref, kc_ref, vc_ref, *rest, n_rows):
    bias_refs, o_ref = rest[:NA_QBLOCKS], rest[NA_QBLOCKS]
    step = pl.program_id(1)
    tq = NA_QROWS * GRID_W
    n_win = NA_KROWS * GRID_W
    low = _low_half(tq)
    key0 = []
    for sb in range(NA_QBLOCKS):
        kb = jnp.clip((step * NA_QBLOCKS + sb) * NA_QROWS - NA_MAX_ROWS // 2, 0, n_rows - NA_KROWS)
        key0.append(pl.multiple_of(kb * GRID_W, GRID_W))

    held = {}

    def scores(sb, hp, half):
        lanes = slice(hp * HEAD_PAIR, (hp + 1) * HEAD_PAIR)
        q_half = _keep_half(q_ref[0, sb * tq:(sb + 1) * tq, lanes], half)
        keys = jnp.concatenate([k_ref[0, pl.ds(key0[sb], n_win), lanes], kc_ref[0, :, lanes]], axis=0)
        s = _nt_dot(q_half, keys)
        bias = bias_refs[sb][0, hp, half * tq:(half + 1) * tq]
        return jnp.concatenate([s[:, :n_win] + bias, s[:, n_win:]], axis=1)

    def finish(sb, hp, half, sc):
        lanes = slice(hp * HEAD_PAIR, (hp + 1) * HEAD_PAIR)
        vals = jnp.concatenate([v_ref[0, pl.ds(key0[sb], n_win), lanes], vc_ref[0, :, lanes]], axis=0)
        o, inv = _softmax_pv(sc, vals)
        o = o * inv
        if half == 0:
            held[(sb, hp)] = o
        else:
            o_ref[0, sb * tq:(sb + 1) * tq, lanes] = jnp.where(low, held.pop((sb, hp)), o).astype(BF16)

    _pipelined([(sb, hp, half) for sb in range(NA_QBLOCKS) for hp in range(NA_HEADS // 2) for half in range(2)],
               scores, finish)


def _na_attention(q, k, v, kc, vc, bias):
    B, S, _ = q.shape
    N = kc.shape[1]
    n_rows = S // GRID_W
    rows_per_step = NA_QBLOCKS * NA_QROWS
    assert S % GRID_W == 0 and n_rows % rows_per_step == 0 and n_rows >= NA_KROWS
    n_blocks = n_rows // NA_QROWS
    tq = rows_per_step * GRID_W
    full = lambda b, t: (b, 0, 0)

    def variant(sb):
        def index(b, t):
            blk = t * NA_QBLOCKS + sb
            return (jnp.where(blk == 0, 0, jnp.where(blk == n_blocks - 1, 2, 1)), 0, 0, 0)
        return index

    return pl.pallas_call(
        functools.partial(_na_kernel, n_rows=n_rows),
        grid=(B, n_rows // rows_per_step),
        in_specs=[pl.BlockSpec((1, tq, D_NA), lambda b, t: (b, t, 0)),
                  pl.BlockSpec((1, S, D_NA), full),
                  pl.BlockSpec((1, S, D_NA), full),
                  pl.BlockSpec((1, N, D_NA), full),
                  pl.BlockSpec((1, N, D_NA), full)]
        + [pl.BlockSpec((1,) + bias.shape[1:], variant(sb)) for sb in range(NA_QBLOCKS)],
        out_specs=pl.BlockSpec((1, tq, D_NA), lambda b, t: (b, t, 0)),
        out_shape=jax.ShapeDtypeStruct((B, S, D_NA), BF16),
        compiler_params=_params(2),
        name="na_attention",
    )(q, k, v, kc, vc, *([bias] * NA_QBLOCKS))


def _na_bias_table(rpb, n_rows):
    j = np.arange(GRID_W)[:, None]
    kc = np.arange(GRID_W)[None, :]
    cs = np.clip(j - NA_COLS // 2, 0, GRID_W - NA_COLS)
    col_ok = (kc >= cs) & (kc < cs + NA_COLS)
    c_off = kc - j + (NA_COLS - 1)
    pick = (np.arange(2 * NA_COLS - 1)[:, None, None] == c_off[None]).astype(np.float32)
    by_off = jnp.einsum("hrc,cjk->hjrk", rpb.astype(F32) * LOG2E, pick, precision=lax.Precision.HIGHEST)
    by_off = jnp.where(col_ok[None, :, None, :], by_off, NEG_BIG)
    by_off = by_off.reshape(NA_HEADS, GRID_W, (2 * NA_MAX_ROWS - 1) * GRID_W)
    tabs = []
    for i0 in (0, NA_QROWS, n_rows - NA_QROWS):
        kb = int(np.clip(i0 - NA_MAX_ROWS // 2, 0, n_rows - NA_KROWS))
        per_row = []
        for i in range(i0, i0 + NA_QROWS):
            rs = int(np.clip(i - NA_MAX_ROWS // 2, 0, n_rows - NA_MAX_ROWS))
            kr_lo, off_lo = rs - kb, rs - i + (NA_MAX_ROWS - 1)
            blk = by_off[:, :, off_lo * GRID_W:(off_lo + NA_MAX_ROWS) * GRID_W]
            per_row.append(jnp.pad(
                blk, ((0, 0), (0, 0), (kr_lo * GRID_W, (NA_KROWS - NA_MAX_ROWS - kr_lo) * GRID_W)),
                constant_values=NEG_BIG))
        tabs.append(jnp.stack(per_row, axis=1).reshape(
            NA_HEADS // 2, 2 * NA_QROWS * GRID_W, NA_KROWS * GRID_W))
    return jnp.stack(tabs)


WA_QBLOCKS = 4


def _wa_kernel(sink_ref, q_ref, k_ref, v_ref, kc_ref, vc_ref, o_ref, *, seq):
    step = pl.program_id(1)
    span = WA_BLOCK + 2 * WA_WINDOW
    low = _low_half(WA_BLOCK)
    row = lax.broadcasted_iota(jnp.int32, (WA_BLOCK, span), 0)
    col = lax.broadcasted_iota(jnp.int32, (WA_BLOCK, span), 1)
    key0, mask_add = [], []
    for sb in range(WA_QBLOCKS):
        n = step * WA_QBLOCKS + sb
        ks = jnp.clip(n * WA_BLOCK - WA_WINDOW, 0, seq - span)
        rel = col + (ks - n * WA_BLOCK) - row
        m = jnp.where(jnp.abs(rel) <= WA_WINDOW, 0.0, NEG_BIG)
        key0.append(pl.multiple_of(ks, WA_BLOCK))
        mask_add.append(jnp.concatenate([m, m], axis=0))

    def kv_lanes(pair):
        kh = pair * 2 // WA_GROUP
        return slice(kh * HEAD_PAIR, (kh + 1) * HEAD_PAIR)

    def scores(sb, pair):
        q_pair = q_ref[0, sb * WA_BLOCK:(sb + 1) * WA_BLOCK, pair * HEAD_PAIR:(pair + 1) * HEAD_PAIR]
        q_stack = jnp.concatenate([_keep_half(q_pair, 0), _keep_half(q_pair, 1)], axis=0)
        keys = jnp.concatenate(
            [k_ref[0, pl.ds(key0[sb], span), kv_lanes(pair)], kc_ref[0, :, kv_lanes(pair)]], axis=0)
        s = _nt_dot(q_stack, keys)
        return jnp.concatenate([s[:, :span] + mask_add[sb], s[:, span:]], axis=1)

    def finish(sb, pair, sc):
        vals = jnp.concatenate(
            [v_ref[0, pl.ds(key0[sb], span), kv_lanes(pair)], vc_ref[0, :, kv_lanes(pair)]], axis=0)
        sinks = [sink_ref[2 * pair + half] * LOG2E for half in range(2)]
        o, inv = _softmax_pv(sc, vals, sinks)
        o = o * inv
        o_ref[0, sb * WA_BLOCK:(sb + 1) * WA_BLOCK, pair * HEAD_PAIR:(pair + 1) * HEAD_PAIR] = jnp.where(
            low, o[:WA_BLOCK], o[WA_BLOCK:]).astype(BF16)

    _pipelined([(sb, pair) for sb in range(WA_QBLOCKS) for pair in range(WA_HEADS // 2)], scores, finish)


def _wa_attention(q, k, v, kc, vc, sink):
    B, S, _ = q.shape
    N = kc.shape[1]
    tq = WA_QBLOCKS * WA_BLOCK
    assert S % tq == 0 and S >= WA_BLOCK + 2 * WA_WINDOW
    full = lambda b, t: (b, 0, 0)
    return pl.pallas_call(
        functools.partial(_wa_kernel, seq=S),
        grid=(B, S // tq),
        in_specs=[pl.BlockSpec(memory_space=pltpu.SMEM),
                  pl.BlockSpec((1, tq, D_WA), lambda b, t: (b, t, 0)),
                  pl.BlockSpec((1, S, 2 * D_WA_KV), full),
                  pl.BlockSpec((1, S, 2 * D_WA_KV), full),
                  pl.BlockSpec((1, N, 2 * D_WA_KV), full),
                  pl.BlockSpec((1, N, 2 * D_WA_KV), full)],
        out_specs=pl.BlockSpec((1, tq, D_WA), lambda b, t: (b, t, 0)),
        out_shape=jax.ShapeDtypeStruct((B, S, D_WA), BF16),
        compiler_params=_params(2),
        name="wa_attention",
    )(sink, q, k, v, kc, vc)


def _ctx_attn_kernel(sink_ref, nq_ref, nk_ref, nv_ref, wq_ref, wk_ref, wv_ref, ona_ref, owa_ref):
    n_ctx = nq_ref.shape[1]
    low = _low_half(n_ctx)
    for hp in range(NA_HEADS // 2):
        lanes = slice(hp * HEAD_PAIR, (hp + 1) * HEAD_PAIR)
        q_pair = nq_ref[0, :, lanes]
        k_c = nk_ref[0, :, lanes]
        v_c = nv_ref[0, :, lanes]
        halves = []
        for half in range(2):
            qh = _keep_half(q_pair, half)
            o, inv = _softmax_pv(_nt_dot(qh, k_c), v_c)
            halves.append(o * inv)
        ona_ref[0, :, lanes] = jnp.where(low, halves[0], halves[1]).astype(BF16)
    for pair in range(WA_HEADS // 2):
        lanes = slice(pair * HEAD_PAIR, (pair + 1) * HEAD_PAIR)
        kh = pair * 2 // WA_GROUP
        kv_lanes = slice(kh * HEAD_PAIR, (kh + 1) * HEAD_PAIR)
        q_pair = wq_ref[0, :, lanes]
        k_c = wk_ref[0, :, kv_lanes]
        v_c = wv_ref[0, :, kv_lanes]
        halves = []
        for half in range(2):
            qh = _keep_half(q_pair, half)
            o, inv = _softmax_pv(_nt_dot(qh, k_c), v_c, [sink_ref[2 * pair + half] * LOG2E])
            halves.append(o * inv)
        owa_ref[0, :, lanes] = jnp.where(low, halves[0], halves[1]).astype(BF16)


def _ctx_attention(nq, nk, nv, wq, wk, wv, sink):
    B, N, _ = nq.shape
    blk = lambda n: pl.BlockSpec((1, N, n), lambda b: (b, 0, 0))
    return pl.pallas_call(
        _ctx_attn_kernel,
        grid=(B,),
        in_specs=[pl.BlockSpec(memory_space=pltpu.SMEM),
                  blk(D_NA), blk(D_NA), blk(D_NA), blk(D_WA), blk(2 * D_WA_KV), blk(2 * D_WA_KV)],
        out_specs=[blk(D_NA), blk(D_WA)],
        out_shape=[jax.ShapeDtypeStruct((B, N, D_NA), BF16), jax.ShapeDtypeStruct((B, N, D_WA), BF16)],
        compiler_params=_params(1),
        name="ctx_attention",
    )(sink, nq, nk, nv, wq, wk, wv)


CONV_HALO = 16
SUBLANES = 8
CONV_CHUNK = 64


def _conv_kernel(a_ref, ga_ref, w_ref, cb_ref, lg_ref, lb_ref, z_ref, pad_scr, shift_scr, y_scr,
                 *, tm, n_tiles):
    j = pl.program_id(1)
    t0 = pl.multiple_of(j * tm, tm)
    pad_scr[CONV_HALO:CONV_HALO + tm, :] = a_ref[0, pl.ds(t0, tm), :].astype(F32)
    zeros = jnp.zeros((CONV_HALO, D_CONV), F32)

    @pl.when(j > 0)
    def _():
        pad_scr[0:CONV_HALO, :] = a_ref[0, pl.ds(t0 - CONV_HALO, CONV_HALO), :].astype(F32)

    @pl.when(j == 0)
    def _():
        pad_scr[0:CONV_HALO, :] = zeros

    @pl.when(j < n_tiles - 1)
    def _():
        pad_scr[CONV_HALO + tm:, :] = a_ref[0, pl.ds(t0 + tm, CONV_HALO), :].astype(F32)

    @pl.when(j == n_tiles - 1)
    def _():
        pad_scr[CONV_HALO + tm:, :] = zeros

    n_shift = tm + CONV_HALO + SUBLANES
    for r in range(1, SUBLANES):
        shift_scr[r - 1] = pad_scr[r:r + n_shift, :]

    def tap(k, row0, lanes):
        r, base = (k + 1) % SUBLANES, (k + 1) // SUBLANES * SUBLANES
        if r == 0:
            return pad_scr[pl.ds(row0 + base, SUBLANES), lanes]
        return shift_scr[r - 1, pl.ds(row0 + base, SUBLANES), lanes]

    groups = CONV_CHUNK // SUBLANES
    for c in range(D_CONV // LANES):
        lanes = slice(c * LANES, (c + 1) * LANES)
        b_rows = jnp.broadcast_to(cb_ref[:, lanes], (SUBLANES, LANES))

        def chunk(g, carry, lanes=lanes, b_rows=b_rows):
            row0 = pl.multiple_of(g * CONV_CHUNK, CONV_CHUNK)
            sums = [[b_rows, None] for _ in range(groups)]
            for k in range(CONV_K):
                w_row = jnp.broadcast_to(w_ref[k:k + 1, lanes], (SUBLANES, LANES))
                for q in range(groups):
                    term = tap(k, row0 + q * SUBLANES, lanes) * w_row
                    prev = sums[q][k % 2]
                    sums[q][k % 2] = term if prev is None else prev + term
            for q in range(groups):
                y_scr[pl.ds(row0 + q * SUBLANES, SUBLANES), lanes] = sums[q][0] + sums[q][1]
            return carry

        lax.fori_loop(0, tm // CONV_CHUNK, chunk, 0)

    acc = y_scr[...]
    mu = jnp.mean(acc, axis=-1, keepdims=True)
    d = acc - mu
    var = jnp.mean(d * d, axis=-1, keepdims=True)
    y = d * lax.rsqrt(var + EPS) * lg_ref[...] + lb_ref[...]
    z_ref[0] = (_silu(y) * ga_ref[0].astype(F32)).astype(BF16)


def _conv_branch(a, ga, conv_w, conv_b, ln_g, ln_b, tm):
    B, T, _ = a.shape
    n_tiles = T // tm
    const2 = lambda b, t: (0, 0)
    row = pl.BlockSpec((1, D_CONV), const2)
    return pl.pallas_call(
        functools.partial(_conv_kernel, tm=tm, n_tiles=n_tiles),
        grid=(B, n_tiles),
        in_specs=[pl.BlockSpec((1, T, D_CONV), lambda b, t: (b, 0, 0)),
                  pl.BlockSpec((1, tm, D_CONV), lambda b, t: (b, t, 0)),
                  pl.BlockSpec((CONV_K, D_CONV), const2),
                  row, row, row],
        out_specs=pl.BlockSpec((1, tm, D_CONV), lambda b, t: (b, t, 0)),
        out_shape=jax.ShapeDtypeStruct((B, T, D_CONV), BF16),
        scratch_shapes=[pltpu.VMEM((tm + 2 * CONV_HALO, D_CONV), F32),
                        pltpu.VMEM((SUBLANES - 1, tm + CONV_HALO + SUBLANES, D_CONV), F32),
                        pltpu.VMEM((tm, D_CONV), F32)],
        compiler_params=_params(2),
        name="conv_branch",
    )(a, ga, conv_w, conv_b, ln_g, ln_b)


def _merge_kernel(z_ref, ona_ref, gn_ref, owa_ref, gw_ref, x_ref, mod_ref, g_ref,
                  wmg_ref, wa_ref, wb_ref, wc_ref, wo_ref, o_ref):
    def gated(o_r, g_r):
        return (o_r[0].astype(F32) * g_r[0].astype(F32)).astype(BF16)

    x = x_ref[0]
    h = _modulated_norm(x, mod_ref, g_ref)
    branches = (jnp.dot(z_ref[0], wa_ref[...], preferred_element_type=F32),
                jnp.dot(gated(ona_ref, gn_ref), wb_ref[...], preferred_element_type=F32),
                jnp.dot(gated(owa_ref, gw_ref), wc_ref[...], preferred_element_type=F32))
    y = None
    for i, yi in enumerate(branches):
        gate_i = _sigmoid(jnp.dot(h, wmg_ref[:, i * D_MODEL:(i + 1) * D_MODEL], preferred_element_type=F32))
        y = gate_i * yi if y is None else y + gate_i * yi
    gate = mod_ref[0, :, 2 * D_MODEL:]
    o_ref[0] = x + gate * jnp.dot(y.astype(BF16), wo_ref[...], preferred_element_type=F32)


def _merge(z, ona, gn, owa, gw, xs, mod, norm_g, wmg, wa, wb, wc, wo, tm):
    B, T, _ = xs.shape
    tok = lambda n: pl.BlockSpec((1, tm, n), lambda b, t: (b, t, 0))
    const2 = lambda b, t: (0, 0)
    once = lambda r, c: pl.BlockSpec((r, c), const2, pipeline_mode=pl.Buffered(1))
    return pl.pallas_call(
        _merge_kernel,
        grid=(B, T // tm),
        in_specs=[tok(D_CONV), tok(D_NA), tok(D_NA), tok(D_WA), tok(D_WA), tok(D_MODEL),
                  pl.BlockSpec((1, 1, 3 * D_MODEL), lambda b, t: (b, 0, 0)),
                  pl.BlockSpec((1, D_MODEL), const2),
                  once(D_MODEL, 3 * D_MODEL), once(D_CONV, D_MODEL), once(D_NA, D_MODEL),
                  once(D_WA, D_MODEL), once(D_MODEL, D_MODEL)],
        out_specs=tok(D_MODEL),
        out_shape=jax.ShapeDtypeStruct((B, T, D_MODEL), F32),
        compiler_params=_params(2),
        name="merge",
    )(z, ona, gn, owa, gw, xs, mod, norm_g, wmg, wa, wb, wc, wo)


def _rope_tables(seq):
    t = np.arange(seq)
    half = HEAD_DIM // 4
    inv = ROPE_BASE ** (-np.arange(half, dtype=np.float32) / half)
    lane = np.arange(LANES)
    d = lane % HEAD_DIM
    use_col = (d // (HEAD_DIM // 2)) == 1
    dd = d % (HEAD_DIM // 2)
    first = dd < half
    pos = np.where(use_col[None, :], (t % GRID_W)[:, None], (t // GRID_W)[:, None]).astype(np.float32)
    ang = pos * inv[dd % half][None, :]
    cos, sin = np.cos(ang), np.sin(ang)
    s1 = np.where(first[None, :], 0.0, sin)
    s2 = np.where(first[None, :], -sin, 0.0)
    return tuple(jnp.asarray(v, F32) for v in (cos, s1, s2))


def _block_diag_ones():
    i = np.arange(MXU_DIM)
    return jnp.asarray((i[:, None] // HEAD_DIM) == (i[None, :] // HEAD_DIM), BF16)


def _pick_tile(n, target):
    t = min(n, target)
    while n % t:
        t //= 2
    return t


def kernel(x, c, ctx, c_ctx, norm_g, w_ada, b_ada, w_in, conv_w, conv_b, cln_g, cln_b, w_proj_a,
           na_q_norm, na_k_norm, na_rpb, w_proj_b, wa_q_norm, wa_k_norm, wa_sink, w_proj_c, w_o):
    B, S, _ = x.shape
    N = ctx.shape[1]
    depth = w_in.shape[0]
    assert B <= 8
    c_all = jnp.zeros((16, D_MODEL), F32).at[:B].set(c).at[B].set(c_ctx)
    rope_tabs = _rope_tables(S)
    bd = _block_diag_ones()
    q_scale = HEAD_DIM ** -0.5 * LOG2E
    tm_x = _pick_tile(S, 1024)
    tm_c = _pick_tile(N, 512)
    tile_row = lambda w, n: jnp.tile(w.astype(F32), n)[None, :]

    for l in range(depth):
        update_ctx = l < depth - 1
        mod = _modulation(c_all, w_ada[l], b_ada[l][None, :])
        mod_x = mod[:B, None, :]
        mod_c = jnp.broadcast_to(mod[B][None, None, :], (B, 1, 3 * D_MODEL))

        w_l = w_in[l][:, :D_PROJ].astype(BF16)
        w_mg = w_in[l][:, D_PROJ:].astype(BF16)
        nqw = tile_row(na_q_norm[l], NA_HEADS) * q_scale
        nkw = tile_row(na_k_norm[l], NA_HEADS)
        wqw = tile_row(wa_q_norm[l], WA_HEADS) * q_scale
        wkw = tile_row(wa_k_norm[l], WA_KV_HEADS)
        g_row = norm_g[l][None, :]

        (a, ga, nq, nk, nv, gn, wq, wk, wv, gw) = _projection(
            x, mod_x, g_row, w_l, nqw, nkw, wqw, wkw, bd, rope_tabs, tm_x)
        if update_ctx:
            (ac, gac, ncq, nkc, nvc, gnc, wcq, wkc, wvc, gwc) = _projection(
                ctx, mod_c, g_row, w_l, nqw, nkw, wqw, wkw, bd, None, tm_c)
        else:
            nkc, nvc, wkc, wvc = _projection(
                ctx, mod_c, g_row, w_l, nqw, nkw, wqw, wkw, bd, None, tm_c, kv_only=True)

        bias = _na_bias_table(na_rpb[l], S // GRID_W)
        sink = wa_sink[l].astype(F32)
        o_na = _na_attention(nq, nk, nv, nkc, nvc, bias)
        o_wa = _wa_attention(wq, wk, wv, wkc, wvc, sink)
        z = _conv_branch(a, ga, conv_w[l], conv_b[l][None, :], cln_g[l][None, :], cln_b[l][None, :],
                         _pick_tile(S, 512))
        wa_b = w_proj_a[l].astype(BF16)
        wb_b = w_proj_b[l].astype(BF16)
        wc_b = w_proj_c[l].astype(BF16)
        wo_b = w_o[l].astype(BF16)
        x_new = _merge(z, o_na, gn, o_wa, gw, x, mod_x, g_row, w_mg, wa_b, wb_b, wc_b, wo_b, tm_x)

        if update_ctx:
            oc_na, oc_wa = _ctx_attention(ncq, nkc, nvc, wcq, wkc, wvc, sink)
            zc = _conv_branch(ac, gac, conv_w[l], conv_b[l][None, :], cln_g[l][None, :], cln_b[l][None, :],
                              _pick_tile(N, 256))
            ctx = _merge(zc, oc_na, gnc, oc_wa, gwc, ctx, mod_c, g_row, w_mg, wa_b, wb_b, wc_b, wo_b, tm_c)
        x = x_new
    return x
```

```python
import functools

import numpy as np
import jax
import jax.numpy as jnp
from jax import lax
from jax.experimental import pallas as pl
from jax.experimental.pallas import tpu as pltpu

D_MODEL = 1024
GRID_W = 64
HEAD_DIM = 64
D_CONV = 512
CONV_K = 31
NA_HEADS = 8
NA_MAX_ROWS = 8
NA_COLS = 16
WA_HEADS = 8
WA_KV_HEADS = 2
WA_GROUP = WA_HEADS // WA_KV_HEADS
WA_WINDOW = 128
WA_BLOCK = 128
ROPE_BASE = 10000.0
EPS = 1e-6
D_NA = NA_HEADS * HEAD_DIM
D_WA = WA_HEADS * HEAD_DIM
D_WA_KV = WA_KV_HEADS * HEAD_DIM
SPLIT_SIZES = (2 * D_CONV, D_CONV, D_NA, D_NA, D_NA, D_NA, D_WA, D_WA_KV, D_WA_KV, D_WA, 3 * D_MODEL)
D_IN = sum(SPLIT_SIZES)
SPLIT_STARTS = tuple(int(v) for v in np.cumsum((0,) + SPLIT_SIZES[:-1]))

LANES = 128
MXU_DIM = 256
HEAD_PAIR = 2 * HEAD_DIM
ROPE_HALF = HEAD_DIM // 4
NEG_BIG = -1e30
LOG2E = 1.4426950408889634
VMEM_LIMIT = 56 * 1024 * 1024

F32 = jnp.float32
BF16 = jnp.bfloat16


def _sigmoid(v):
    return 1.0 / (1.0 + jnp.exp(-v))


def _silu(v):
    return v * _sigmoid(v)


def _params(n_grid):
    return pltpu.CompilerParams(dimension_semantics=("arbitrary",) * n_grid,
                                vmem_limit_bytes=VMEM_LIMIT)


def _mod_kernel(c_ref, w_ref, b_ref, o_ref):
    s = _silu(c_ref[...])
    o_ref[...] = jnp.dot(s.astype(BF16), w_ref[...].astype(BF16),
                         preferred_element_type=F32) + b_ref[...]


def _modulation(c_all, w_ada, b_ada, layer):
    rows = c_all.shape[0]
    n_out = w_ada.shape[2]
    blk = D_MODEL
    return pl.pallas_call(
        _mod_kernel,
        grid=(n_out // blk,),
        in_specs=[pl.BlockSpec((rows, D_MODEL), lambda j: (0, 0)),
                  pl.BlockSpec((None, D_MODEL, blk), lambda j: (layer, 0, j)),
                  pl.BlockSpec((None, 1, blk), lambda j: (layer, 0, j))],
        out_specs=pl.BlockSpec((rows, blk), lambda j: (0, j)),
        out_shape=jax.ShapeDtypeStruct((rows, n_out), F32),
        compiler_params=_params(1),
        name="modulation",
    )(c_all, w_ada, b_ada)


PROJ_SUB = 512
D_PROJ = SPLIT_STARTS[10]


def _modulated_norm(x, mod_ref, g_ref):
    ms = jnp.mean(x * x, axis=-1, keepdims=True)
    shift = mod_ref[0, :, 0:D_MODEL]
    scale = mod_ref[0, :, D_MODEL:2 * D_MODEL]
    return ((x * lax.rsqrt(ms + EPS) * g_ref[...]) * (1.0 + scale) + shift).astype(BF16)


def _proj_kernel(*refs, rope, kv_only):
    if rope:
        (x_ref, mod_ref, g_ref, w_ref, nqw_ref, nkw_ref, wqw_ref, wkw_ref, bd_ref,
         cos_ref, s1_ref, s2_ref) = refs[:12]
        outs = refs[12:]
    else:
        (x_ref, mod_ref, g_ref, w_ref, nqw_ref, nkw_ref, wqw_ref, wkw_ref, bd_ref) = refs[:9]
        cos_ref = s1_ref = s2_ref = None
        outs = refs[9:]
    if kv_only:
        nk_ref, nv_ref, wk_ref, wv_ref, h_scr = outs
    else:
        (a_ref, ga_ref, nq_ref, nk_ref, nv_ref, gn_ref, wq_ref, wk_ref, wv_ref, gw_ref,
         h_scr) = outs

    st = SPLIT_STARTS
    tm = x_ref.shape[1]
    sub = min(tm, PROJ_SUB)
    for r0 in range(0, tm, sub):
        h_scr[r0:r0 + sub, :] = _modulated_norm(x_ref[0, r0:r0 + sub, :], mod_ref, g_ref)

    def head_norm(t, w_row_ref):
        n = t.shape[1]
        sq = (t * t).astype(BF16)
        if n >= MXU_DIM:
            parts = [jnp.dot(sq[:, c:c + MXU_DIM], bd_ref[...], preferred_element_type=F32)
                     for c in range(0, n, MXU_DIM)]
            ss = jnp.concatenate(parts, axis=1) if len(parts) > 1 else parts[0]
        else:
            ss = jnp.dot(sq, bd_ref[0:n, 0:n], preferred_element_type=F32)
        return t * lax.rsqrt(ss * (1.0 / HEAD_DIM) + EPS) * w_row_ref[...]

    low = _low_half(sub)

    for r0 in range(0, tm, sub):
        rows = slice(r0, r0 + sub)

        def seg(c0, n):
            return jnp.dot(h_scr[rows, :], w_ref[:, c0:c0 + n], preferred_element_type=F32)

        def rotary(tc):
            if not rope:
                return tc
            lo = pltpu.roll(tc, ROPE_HALF, axis=1)
            hi = pltpu.roll(tc, LANES - ROPE_HALF, axis=1)
            return tc * cos_ref[rows, :] + lo * s1_ref[rows, :] + hi * s2_ref[rows, :]

        def store_kv_dup(t, o_ref):
            sw = pltpu.roll(t, HEAD_DIM, axis=1)
            o_ref[0, rows, 0:LANES] = jnp.where(low, t, sw).astype(BF16)
            o_ref[0, rows, LANES:2 * LANES] = jnp.where(low, sw, t).astype(BF16)

        nk_ref[0, rows, :] = head_norm(seg(st[3], D_NA), nkw_ref).astype(BF16)
        nv_ref[0, rows, :] = seg(st[4], D_NA).astype(BF16)
        store_kv_dup(rotary(head_norm(seg(st[7], D_WA_KV), wkw_ref)), wk_ref)
        store_kv_dup(seg(st[8], D_WA_KV), wv_ref)
        if kv_only:
            continue
        glu = seg(st[0], 2 * D_CONV)
        a_ref[0, rows, :] = (glu[:, :D_CONV] * _sigmoid(glu[:, D_CONV:])).astype(BF16)
        ga_ref[0, rows, :] = _silu(seg(st[1], D_CONV)).astype(BF16)
        nq_ref[0, rows, :] = head_norm(seg(st[2], D_NA), nqw_ref).astype(BF16)
        gn_ref[0, rows, :] = _silu(seg(st[5], D_NA)).astype(BF16)
        wq = head_norm(seg(st[6], D_WA), wqw_ref)
        for c in range(0, D_WA, LANES):
            wq_ref[0, rows, c:c + LANES] = rotary(wq[:, c:c + LANES]).astype(BF16)
        gw_ref[0, rows, :] = _silu(seg(st[9], D_WA)).astype(BF16)


_PROJ_OUT_WIDTHS = (D_CONV, D_CONV, D_NA, D_NA, D_NA, D_NA, D_WA, 2 * D_WA_KV, 2 * D_WA_KV, D_WA)
_PROJ_KV_WIDTHS = (D_NA, D_NA, 2 * D_WA_KV, 2 * D_WA_KV)


def _projection(xs, mod, norm_g, w_in, layer, nqw, nkw, wqw, wkw, bd, rope_tabs, tm, kv_only=False):
    B, T, _ = xs.shape
    rope = rope_tabs is not None
    widths = _PROJ_KV_WIDTHS if kv_only else _PROJ_OUT_WIDTHS
    const2 = lambda b, t: (0, 0)
    row_spec = lambda n: pl.BlockSpec((1, n), const2)
    in_specs = [
        pl.BlockSpec((1, tm, D_MODEL), lambda b, t: (b, t, 0)),
        pl.BlockSpec((1, 1, 3 * D_MODEL), lambda b, t: (b, 0, 0)),
        row_spec(D_MODEL),
        pl.BlockSpec((None, D_MODEL, D_PROJ), lambda b, t: (layer, 0, 0), pipeline_mode=pl.Buffered(1)),
        row_spec(D_NA), row_spec(D_NA), row_spec(D_WA), row_spec(D_WA_KV),
        pl.BlockSpec((MXU_DIM, MXU_DIM), const2),
    ]
    args = [xs, mod, norm_g, w_in, nqw, nkw, wqw, wkw, bd]
    if rope:
        in_specs += [pl.BlockSpec((tm, LANES), lambda b, t: (t, 0))] * 3
        args += list(rope_tabs)
    out_specs = [pl.BlockSpec((1, tm, n), lambda b, t: (b, t, 0)) for n in widths]
    out_shape = [jax.ShapeDtypeStruct((B, T, n), BF16) for n in widths]
    return pl.pallas_call(
        functools.partial(_proj_kernel, rope=rope, kv_only=kv_only),
        grid=(B, T // tm),
        in_specs=in_specs,
        out_specs=out_specs,
        out_shape=out_shape,
        scratch_shapes=[pltpu.VMEM((tm, D_MODEL), BF16)],
        compiler_params=_params(2),
        name="projection_rope" if rope else ("projection_ctx_kv" if kv_only else "projection_ctx"),
    )(*args)


def _nt_dot(a, b):
    return lax.dot_general(a, b, (((1,), (1,)), ((), ())), preferred_element_type=F32)


def _low_half(rows):
    return lax.broadcasted_iota(jnp.int32, (rows, HEAD_PAIR), 1) < HEAD_DIM


def _keep_half(q_pair, half):
    lane = lax.broadcasted_iota(jnp.int32, (1, HEAD_PAIR), 1)
    keep = (lane < HEAD_DIM) if half == 0 else (lane >= HEAD_DIM)
    return q_pair * keep.astype(F32).astype(q_pair.dtype)


def _softmax_pv(scores, values, sinks=None):
    rows = scores.shape[0]
    n_chunks = len(sinks) if sinks else 1
    chunk = rows // n_chunks
    p_chunks, inv_chunks = [], []
    for c in range(n_chunks):
        s = scores[c * chunk:(c + 1) * chunk]
        m = jnp.max(s, axis=-1, keepdims=True)
        if sinks:
            m = jnp.maximum(m, sinks[c])
        p = jnp.exp2(s - m)
        den = jnp.sum(p, axis=-1, keepdims=True)
        if sinks:
            den = den + jnp.exp2(sinks[c] - m)
        p_chunks.append(p.astype(BF16))
        inv_chunks.append(1.0 / den)
    p = p_chunks[0] if n_chunks == 1 else jnp.concatenate(p_chunks, axis=0)
    inv = inv_chunks[0] if n_chunks == 1 else jnp.concatenate(inv_chunks, axis=0)
    return jnp.dot(p, values, preferred_element_type=F32), inv


PIPELINE_AHEAD = 1


def _pipelined(problems, scores_fn, finish_fn):
    pending = [scores_fn(*p) for p in problems[:PIPELINE_AHEAD]]
    for i, prob in enumerate(problems):
        if i + PIPELINE_AHEAD < len(problems):
            pending.append(scores_fn(*problems[i + PIPELINE_AHEAD]))
        finish_fn(*prob, pending.pop(0))


NA_QROWS = 4
NA_KROWS = NA_QROWS + NA_MAX_ROWS


NA_QBLOCKS = 2


def _na_kernel(q_ref, k_ref, v_ref, kc_ref, vc_ref, *rest, n_rows):
    bias_refs, o_ref = rest[:NA_QBLOCKS], rest[NA_QBLOCKS]
    step = pl.program_id(1)
    tq = NA_QROWS * GRID_W
    n_win = NA_KROWS * GRID_W
    low = _low_half(tq)
    key0 = []
    for sb in range(NA_QBLOCKS):
        kb = jnp.clip((step * NA_QBLOCKS + sb) * NA_QROWS - NA_MAX_ROWS // 2, 0, n_rows - NA_KROWS)
        key0.append(pl.multiple_of(kb * GRID_W, GRID_W))

    held = {}

    def scores(sb, hp, half):
        lanes = slice(hp * HEAD_PAIR, (hp + 1) * HEAD_PAIR)
        q_half = _keep_half(q_ref[0, sb * tq:(sb + 1) * tq, lanes], half)
        keys = jnp.concatenate([k_ref[0, pl.ds(key0[sb], n_win), lanes], kc_ref[0, :, lanes]], axis=0)
        s = _nt_dot(q_half, keys)
        bias = bias_refs[sb][0, hp, half * tq:(half + 1) * tq]
        return jnp.concatenate([s[:, :n_win] + bias, s[:, n_win:]], axis=1)

    def finish(sb, hp, half, sc):
        lanes = slice(hp * HEAD_PAIR, (hp + 1) * HEAD_PAIR)
        vals = jnp.concatenate([v_ref[0, pl.ds(key0[sb], n_win), lanes], vc_ref[0, :, lanes]], axis=0)
        o, inv = _softmax_pv(sc, vals)
        o = o * inv
        if half == 0:
            held[(sb, hp)] = o
        else:
            o_ref[0, sb * tq:(sb + 1) * tq, lanes] = jnp.where(low, held.pop((sb, hp)), o).astype(BF16)

    _pipelined([(sb, hp, half) for sb in range(NA_QBLOCKS) for hp in range(NA_HEADS // 2) for half in range(2)],
               scores, finish)


def _na_attention(q, k, v, kc, vc, bias):
    B, S, _ = q.shape
    N = kc.shape[1]
    n_rows = S // GRID_W
    rows_per_step = NA_QBLOCKS * NA_QROWS
    assert S % GRID_W == 0 and n_rows % rows_per_step == 0 and n_rows >= NA_KROWS
    n_blocks = n_rows // NA_QROWS
    tq = rows_per_step * GRID_W
    full = lambda b, t: (b, 0, 0)

    def variant(sb):
        def index(b, t):
            blk = t * NA_QBLOCKS + sb
            return (jnp.where(blk == 0, 0, jnp.where(blk == n_blocks - 1, 2, 1)), 0, 0, 0)
        return index

    return pl.pallas_call(
        functools.partial(_na_kernel, n_rows=n_rows),
        grid=(B, n_rows // rows_per_step),
        in_specs=[pl.BlockSpec((1, tq, D_NA), lambda b, t: (b, t, 0)),
                  pl.BlockSpec((1, S, D_NA), full),
                  pl.BlockSpec((1, S, D_NA), full),
                  pl.BlockSpec((1, N, D_NA), full),
                  pl.BlockSpec((1, N, D_NA), full)]
        + [pl.BlockSpec((1,) + bias.shape[1:], variant(sb)) for sb in range(NA_QBLOCKS)],
        out_specs=pl.BlockSpec((1, tq, D_NA), lambda b, t: (b, t, 0)),
        out_shape=jax.ShapeDtypeStruct((B, S, D_NA), BF16),
        compiler_params=_params(2),
        name="na_attention",
    )(q, k, v, kc, vc, *([bias] * NA_QBLOCKS))


def _na_bias_table(rpb, n_rows):
    j = np.arange(GRID_W)[:, None]
    kc = np.arange(GRID_W)[None, :]
    cs = np.clip(j - NA_COLS // 2, 0, GRID_W - NA_COLS)
    col_ok = (kc >= cs) & (kc < cs + NA_COLS)
    c_off = kc - j + (NA_COLS - 1)
    pick = (np.arange(2 * NA_COLS - 1)[:, None, None] == c_off[None]).astype(np.float32)
    by_off = jnp.einsum("hrc,cjk->hjrk", rpb.astype(F32) * LOG2E, pick, precision=lax.Precision.HIGHEST)
    by_off = jnp.where(col_ok[None, :, None, :], by_off, NEG_BIG)
    by_off = by_off.reshape(NA_HEADS, GRID_W, (2 * NA_MAX_ROWS - 1) * GRID_W)
    tabs = []
    for i0 in (0, NA_QROWS, n_rows - NA_QROWS):
        kb = int(np.clip(i0 - NA_MAX_ROWS // 2, 0, n_rows - NA_KROWS))
        per_row = []
        for i in range(i0, i0 + NA_QROWS):
            rs = int(np.clip(i - NA_MAX_ROWS // 2, 0, n_rows - NA_MAX_ROWS))
            kr_lo, off_lo = rs - kb, rs - i + (NA_MAX_ROWS - 1)
            blk = by_off[:, :, off_lo * GRID_W:(off_lo + NA_MAX_ROWS) * GRID_W]
            per_row.append(jnp.pad(
                blk, ((0, 0), (0, 0), (kr_lo * GRID_W, (NA_KROWS - NA_MAX_ROWS - kr_lo) * GRID_W)),
                constant_values=NEG_BIG))
        tabs.append(jnp.stack(per_row, axis=1).reshape(
            NA_HEADS // 2, 2 * NA_QROWS * GRID_W, NA_KROWS * GRID_W))
    return jnp.stack(tabs)


WA_QBLOCKS = 4


def _wa_kernel(sink_ref, q_ref, k_ref, v_ref, kc_ref, vc_ref, o_ref, *, seq):
    step = pl.program_id(1)
    span = WA_BLOCK + 2 * WA_WINDOW
    low = _low_half(WA_BLOCK)
    row = lax.broadcasted_iota(jnp.int32, (WA_BLOCK, span), 0)
    col = lax.broadcasted_iota(jnp.int32, (WA_BLOCK, span), 1)
    key0, mask_add = [], []
    for sb in range(WA_QBLOCKS):
        n = step * WA_QBLOCKS + sb
        ks = jnp.clip(n * WA_BLOCK - WA_WINDOW, 0, seq - span)
        rel = col + (ks - n * WA_BLOCK) - row
        m = jnp.where(jnp.abs(rel) <= WA_WINDOW, 0.0, NEG_BIG)
        key0.append(pl.multiple_of(ks, WA_BLOCK))
        mask_add.append(jnp.concatenate([m, m], axis=0))

    def kv_lanes(pair):
        kh = pair * 2 // WA_GROUP
        return slice(kh * HEAD_PAIR, (kh + 1) * HEAD_PAIR)

    def scores(sb, pair):
        q_pair = q_ref[0, sb * WA_BLOCK:(sb + 1) * WA_BLOCK, pair * HEAD_PAIR:(pair + 1) * HEAD_PAIR]
        q_stack = jnp.concatenate([_keep_half(q_pair, 0), _keep_half(q_pair, 1)], axis=0)
        keys = jnp.concatenate(
            [k_ref[0, pl.ds(key0[sb], span), kv_lanes(pair)], kc_ref[0, :, kv_lanes(pair)]], axis=0)
        s = _nt_dot(q_stack, keys)
        return jnp.concatenate([s[:, :span] + mask_add[sb], s[:, span:]], axis=1)

    def finish(sb, pair, sc):
        vals = jnp.concatenate(
            [v_ref[0, pl.ds(key0[sb], span), kv_lanes(pair)], vc_ref[0, :, kv_lanes(pair)]], axis=0)
        sinks = [sink_ref[2 * pair + half] * LOG2E for half in range(2)]
        o, inv = _softmax_pv(sc, vals, sinks)
        o = o * inv
        o_ref[0, sb * WA_BLOCK:(sb + 1) * WA_BLOCK, pair * HEAD_PAIR:(pair + 1) * HEAD_PAIR] = jnp.where(
            low, o[:WA_BLOCK], o[WA_BLOCK:]).astype(BF16)

    _pipelined([(sb, pair) for sb in range(WA_QBLOCKS) for pair in range(WA_HEADS // 2)], scores, finish)


def _wa_attention(q, k, v, kc, vc, sink):
    B, S, _ = q.shape
    N = kc.shape[1]
    tq = WA_QBLOCKS * WA_BLOCK
    assert S % tq == 0 and S >= WA_BLOCK + 2 * WA_WINDOW
    full = lambda b, t: (b, 0, 0)
    return pl.pallas_call(
        functools.partial(_wa_kernel, seq=S),
        grid=(B, S // tq),
        in_specs=[pl.BlockSpec(memory_space=pltpu.SMEM),
                  pl.BlockSpec((1, tq, D_WA), lambda b, t: (b, t, 0)),
                  pl.BlockSpec((1, S, 2 * D_WA_KV), full),
                  pl.BlockSpec((1, S, 2 * D_WA_KV), full),
                  pl.BlockSpec((1, N, 2 * D_WA_KV), full),
                  pl.BlockSpec((1, N, 2 * D_WA_KV), full)],
        out_specs=pl.BlockSpec((1, tq, D_WA), lambda b, t: (b, t, 0)),
        out_shape=jax.ShapeDtypeStruct((B, S, D_WA), BF16),
        compiler_params=_params(2),
        name="wa_attention",
    )(sink, q, k, v, kc, vc)


def _ctx_attn_kernel(sink_ref, nq_ref, nk_ref, nv_ref, wq_ref, wk_ref, wv_ref, ona_ref, owa_ref):
    n_ctx = nq_ref.shape[1]
    low = _low_half(n_ctx)
    for hp in range(NA_HEADS // 2):
        lanes = slice(hp * HEAD_PAIR, (hp + 1) * HEAD_PAIR)
        q_pair = nq_ref[0, :, lanes]
        k_c = nk_ref[0, :, lanes]
        v_c = nv_ref[0, :, lanes]
        halves = []
        for half in range(2):
            qh = _keep_half(q_pair, half)
            o, inv = _softmax_pv(_nt_dot(qh, k_c), v_c)
            halves.append(o * inv)
        ona_ref[0, :, lanes] = jnp.where(low, halves[0], halves[1]).astype(BF16)
    for pair in range(WA_HEADS // 2):
        lanes = slice(pair * HEAD_PAIR, (pair + 1) * HEAD_PAIR)
        kh = pair * 2 // WA_GROUP
        kv_lanes = slice(kh * HEAD_PAIR, (kh + 1) * HEAD_PAIR)
        q_pair = wq_ref[0, :, lanes]
        k_c = wk_ref[0, :, kv_lanes]
        v_c = wv_ref[0, :, kv_lanes]
        halves = []
        for half in range(2):
            qh = _keep_half(q_pair, half)
            o, inv = _softmax_pv(_nt_dot(qh, k_c), v_c, [sink_ref[2 * pair + half] * LOG2E])
            halves.append(o * inv)
        owa_ref[0, :, lanes] = jnp.where(low, halves[0], halves[1]).astype(BF16)


def _ctx_attention(nq, nk, nv, wq, wk, wv, sink):
    B, N, _ = nq.shape
    blk = lambda n: pl.BlockSpec((1, N, n), lambda b: (b, 0, 0))
    return pl.pallas_call(
        _ctx_attn_kernel,
        grid=(B,),
        in_specs=[pl.BlockSpec(memory_space=pltpu.SMEM),
                  blk(D_NA), blk(D_NA), blk(D_NA), blk(D_WA), blk(2 * D_WA_KV), blk(2 * D_WA_KV)],
        out_specs=[blk(D_NA), blk(D_WA)],
        out_shape=[jax.ShapeDtypeStruct((B, N, D_NA), BF16), jax.ShapeDtypeStruct((B, N, D_WA), BF16)],
        compiler_params=_params(1),
        name="ctx_attention",
    )(sink, nq, nk, nv, wq, wk, wv)


CONV_HALO = 16
SUBLANES = 8
CONV_CHUNK = 64


def _conv_kernel(a_ref, ga_ref, w_ref, cb_ref, lg_ref, lb_ref, z_ref, pad_scr, shift_scr, y_scr,
                 *, tm, n_tiles):
    j = pl.program_id(1)
    t0 = pl.multiple_of(j * tm, tm)
    pad_scr[CONV_HALO:CONV_HALO + tm, :] = a_ref[0, pl.ds(t0, tm), :].astype(F32)
    zeros = jnp.zeros((CONV_HALO, D_CONV), F32)

    @pl.when(j > 0)
    def _():
        pad_scr[0:CONV_HALO, :] = a_ref[0, pl.ds(t0 - CONV_HALO, CONV_HALO), :].astype(F32)

    @pl.when(j == 0)
    def _():
        pad_scr[0:CONV_HALO, :] = zeros

    @pl.when(j < n_tiles - 1)
    def _():
        pad_scr[CONV_HALO + tm:, :] = a_ref[0, pl.ds(t0 + tm, CONV_HALO), :].astype(F32)

    @pl.when(j == n_tiles - 1)
    def _():
        pad_scr[CONV_HALO + tm:, :] = zeros

    n_shift = tm + CONV_HALO + SUBLANES
    for r in range(1, SUBLANES):
        shift_scr[r - 1] = pad_scr[r:r + n_shift, :]

    def tap(k, row0, lanes):
        r, base = (k + 1) % SUBLANES, (k + 1) // SUBLANES * SUBLANES
        if r == 0:
            return pad_scr[pl.ds(row0 + base, SUBLANES), lanes]
        return shift_scr[r - 1, pl.ds(row0 + base, SUBLANES), lanes]

    groups = CONV_CHUNK // SUBLANES
    for c in range(D_CONV // LANES):
        lanes = slice(c * LANES, (c + 1) * LANES)
        b_rows = jnp.broadcast_to(cb_ref[:, lanes], (SUBLANES, LANES))

        def chunk(g, carry, lanes=lanes, b_rows=b_rows):
            row0 = pl.multiple_of(g * CONV_CHUNK, CONV_CHUNK)
            sums = [[b_rows, None] for _ in range(groups)]
            for k in range(CONV_K):
                w_row = jnp.broadcast_to(w_ref[k:k + 1, lanes], (SUBLANES, LANES))
                for q in range(groups):
                    term = tap(k, row0 + q * SUBLANES, lanes) * w_row
                    prev = sums[q][k % 2]
                    sums[q][k % 2] = term if prev is None else prev + term
            for q in range(groups):
                y_scr[pl.ds(row0 + q * SUBLANES, SUBLANES), lanes] = sums[q][0] + sums[q][1]
            return carry

        lax.fori_loop(0, tm // CONV_CHUNK, chunk, 0)

    acc = y_scr[...]
    mu = jnp.mean(acc, axis=-1, keepdims=True)
    d = acc - mu
    var = jnp.mean(d * d, axis=-1, keepdims=True)
    y = d * lax.rsqrt(var + EPS) * lg_ref[...] + lb_ref[...]
    z_ref[0] = (_silu(y) * ga_ref[0].astype(F32)).astype(BF16)


def _conv_branch(a, ga, conv_w, conv_b, ln_g, ln_b, tm):
    B, T, _ = a.shape
    n_tiles = T // tm
    const2 = lambda b, t: (0, 0)
    row = pl.BlockSpec((1, D_CONV), const2)
    return pl.pallas_call(
        functools.partial(_conv_kernel, tm=tm, n_tiles=n_tiles),
        grid=(B, n_tiles),
        in_specs=[pl.BlockSpec((1, T, D_CONV), lambda b, t: (b, 0, 0)),
                  pl.BlockSpec((1, tm, D_CONV), lambda b, t: (b, t, 0)),
                  pl.BlockSpec((CONV_K, D_CONV), const2),
                  row, row, row],
        out_specs=pl.BlockSpec((1, tm, D_CONV), lambda b, t: (b, t, 0)),
        out_shape=jax.ShapeDtypeStruct((B, T, D_CONV), BF16),
        scratch_shapes=[pltpu.VMEM((tm + 2 * CONV_HALO, D_CONV), F32),
                        pltpu.VMEM((SUBLANES - 1, tm + CONV_HALO + SUBLANES, D_CONV), F32),
                        pltpu.VMEM((tm, D_CONV), F32)],
        compiler_params=_params(2),
        name="conv_branch",
    )(a, ga, conv_w, conv_b, ln_g, ln_b)


def _merge_kernel(z_ref, ona_ref, gn_ref, owa_ref, gw_ref, x_ref, mod_ref, g_ref,
                  wmg_ref, wa_ref, wb_ref, wc_ref, wo_ref, o_ref):
    def gated(o_r, g_r):
        return (o_r[0].astype(F32) * g_r[0].astype(F32)).astype(BF16)

    x = x_ref[0]
    h = _modulated_norm(x, mod_ref, g_ref)
    branches = (jnp.dot(z_ref[0], wa_ref[...], preferred_element_type=F32),
                jnp.dot(gated(ona_ref, gn_ref), wb_ref[...], preferred_element_type=F32),
                jnp.dot(gated(owa_ref, gw_ref), wc_ref[...], preferred_element_type=F32))
    y = None
    for i, yi in enumerate(branches):
        gate_i = _sigmoid(jnp.dot(h, wmg_ref[:, i * D_MODEL:(i + 1) * D_MODEL], preferred_element_type=F32))
        y = gate_i * yi if y is None else y + gate_i * yi
    gate = mod_ref[0, :, 2 * D_MODEL:]
    o_ref[0] = x + gate * jnp.dot(y.astype(BF16), wo_ref[...], preferred_element_type=F32)


def _merge(z, ona, gn, owa, gw, xs, mod, norm_g, wmg, wa, wb, wc, wo, tm):
    B, T, _ = xs.shape
    tok = lambda n: pl.BlockSpec((1, tm, n), lambda b, t: (b, t, 0))
    const2 = lambda b, t: (0, 0)
    once = lambda r, c: pl.BlockSpec((r, c), const2, pipeline_mode=pl.Buffered(1))
    return pl.pallas_call(
        _merge_kernel,
        grid=(B, T // tm),
        in_specs=[tok(D_CONV), tok(D_NA), tok(D_NA), tok(D_WA), tok(D_WA), tok(D_MODEL),
                  pl.BlockSpec((1, 1, 3 * D_MODEL), lambda b, t: (b, 0, 0)),
                  pl.BlockSpec((1, D_MODEL), const2),
                  once(D_MODEL, 3 * D_MODEL), once(D_CONV, D_MODEL), once(D_NA, D_MODEL),
                  once(D_WA, D_MODEL), once(D_MODEL, D_MODEL)],
        out_specs=tok(D_MODEL),
        out_shape=jax.ShapeDtypeStruct((B, T, D_MODEL), F32),
        compiler_params=_params(2),
        name="merge",
    )(z, ona, gn, owa, gw, xs, mod, norm_g, wmg, wa, wb, wc, wo)


def _rope_tables(seq):
    t = np.arange(seq)
    half = HEAD_DIM // 4
    inv = ROPE_BASE ** (-np.arange(half, dtype=np.float32) / half)
    lane = np.arange(LANES)
    d = lane % HEAD_DIM
    use_col = (d // (HEAD_DIM // 2)) == 1
    dd = d % (HEAD_DIM // 2)
    first = dd < half
    pos = np.where(use_col[None, :], (t % GRID_W)[:, None], (t // GRID_W)[:, None]).astype(np.float32)
    ang = pos * inv[dd % half][None, :]
    cos, sin = np.cos(ang), np.sin(ang)
    s1 = np.where(first[None, :], 0.0, sin)
    s2 = np.where(first[None, :], -sin, 0.0)
    return tuple(jnp.asarray(v, F32) for v in (cos, s1, s2))


def _block_diag_ones():
    i = np.arange(MXU_DIM)
    return jnp.asarray((i[:, None] // HEAD_DIM) == (i[None, :] // HEAD_DIM), BF16)


def _pick_tile(n, target):
    t = min(n, target)
    while n % t:
        t //= 2
    return t


def kernel(x, c, ctx, c_ctx, norm_g, w_ada, b_ada, w_in, conv_w, conv_b, cln_g, cln_b, w_proj_a,
           na_q_norm, na_k_norm, na_rpb, w_proj_b, wa_q_norm, wa_k_norm, wa_sink, w_proj_c, w_o):
    B, S, _ = x.shape
    N = ctx.shape[1]
    depth = w_in.shape[0]
    assert B <= 8
    c_all = jnp.zeros((16, D_MODEL), F32).at[:B].set(c).at[B].set(c_ctx)
    rope_tabs = _rope_tables(S)
    bd = _block_diag_ones()
    q_scale = HEAD_DIM ** -0.5 * LOG2E
    tm_x = _pick_tile(S, 1024)
    tm_c = _pick_tile(N, 512)
    tile_row = lambda w, n: jnp.tile(w.astype(F32), n)[None, :]
    w_bf = w_in.astype(BF16)

    for l in range(depth):
        update_ctx = l < depth - 1
        mod = _modulation(c_all, w_ada, b_ada[:, None, :], l)
        mod_x = mod[:B, None, :]
        mod_c = jnp.broadcast_to(mod[B][None, None, :], (B, 1, 3 * D_MODEL))

        nqw = tile_row(na_q_norm[l], NA_HEADS) * q_scale
        nkw = tile_row(na_k_norm[l], NA_HEADS)
        wqw = tile_row(wa_q_norm[l], WA_HEADS) * q_scale
        wkw = tile_row(wa_k_norm[l], WA_KV_HEADS)
        g_row = norm_g[l][None, :]

        (a, ga, nq, nk, nv, gn, wq, wk, wv, gw) = _projection(
            x, mod_x, g_row, w_bf, l, nqw, nkw, wqw, wkw, bd, rope_tabs, tm_x)
        if update_ctx:
            (ac, gac, ncq, nkc, nvc, gnc, wcq, wkc, wvc, gwc) = _projection(
                ctx, mod_c, g_row, w_bf, l, nqw, nkw, wqw, wkw, bd, None, tm_c)
        else:
            nkc, nvc, wkc, wvc = _projection(
                ctx, mod_c, g_row, w_bf, l, nqw, nkw, wqw, wkw, bd, None, tm_c, kv_only=True)

        bias = _na_bias_table(na_rpb[l], S // GRID_W)
        sink = wa_sink[l].astype(F32)
        o_na = _na_attention(nq, nk, nv, nkc, nvc, bias)
        o_wa = _wa_attention(wq, wk, wv, wkc, wvc, sink)
        z = _conv_branch(a, ga, conv_w[l], conv_b[l][None, :], cln_g[l][None, :], cln_b[l][None, :],
                         _pick_tile(S, 512))
        w_mg = w_bf[l, :, D_PROJ:]
        wa_b = w_proj_a[l].astype(BF16)
        wb_b = w_proj_b[l].astype(BF16)
        wc_b = w_proj_c[l].astype(BF16)
        wo_b = w_o[l].astype(BF16)
        x_new = _merge(z, o_na, gn, o_wa, gw, x, mod_x, g_row, w_mg, wa_b, wb_b, wc_b, wo_b, tm_x)

        if update_ctx:
            oc_na, oc_wa = _ctx_attention(ncq, nkc, nvc, wcq, wkc, wvc, sink)
            zc = _conv_branch(ac, gac, conv_w[l], conv_b[l][None, :], cln_g[l][None, :], cln_b[l][None, :],
                              _pick_tile(N, 256))
            ctx = _merge(zc, oc_na, gnc, oc_wa, gwc, ctx, mod_c, g_row, w_mg, wa_b, wb_b, wc_b, wo_b, tm_c)
        x = x_new
    return x
```
